```python
import math
import jax, jax.numpy as jnp
from jax import lax
import numpy as np

D_MODEL = 1024
BATCH = 8
SEQ = 8192
DEPTH = 1

SSM_EXPAND = 2
SSM_D_INNER = SSM_EXPAND * D_MODEL
SSM_HEAD_DIM = 64
SSM_N_HEADS = SSM_D_INNER // SSM_HEAD_DIM
SSM_N_GROUPS = 8
SSM_D_STATE = 128
SSM_CONV = 5
SSM_CHUNK = 128
SSM_CONV_DIM = SSM_D_INNER + 2 * SSM_N_GROUPS * SSM_D_STATE
MLA_N_HEADS = 16
MLA_Q_RANK = 256
MLA_KV_RANK = 256
MLA_NOPE = 64
MLA_ROPE = 32
MLA_V = 64
MLA_QK = MLA_NOPE + MLA_ROPE
ROPE_BASE = 10000.0
ATTN_BLOCK = 128
D_FF = 4 * D_MODEL
N_BRANCHES = 2
EPS = 1e-6
IN_SPLITS = (SSM_D_INNER, SSM_CONV_DIM, 2 * SSM_N_HEADS, MLA_Q_RANK, MLA_KV_RANK, MLA_ROPE, N_BRANCHES * D_MODEL)
D_IN_PROJ = sum(IN_SPLITS)

kernel_name = "hybrid_bissd_mla_gated_sqrelu"


def rms_norm(x, w):
    xf = x.astype(jnp.float32)
    y = xf * lax.rsqrt(jnp.mean(xf * xf, axis=-1, keepdims=True) + EPS)
    return (y * w.astype(jnp.float32)).astype(x.dtype)


def split_cols(t, sizes):
    idx = [int(i) for i in np.cumsum(sizes)[:-1]]
    return jnp.split(t, idx, axis=-1)


def centred_depthwise_conv(x, w, b):
    y = lax.conv_general_dilated(
        x, w[:, None, :].astype(x.dtype), window_strides=(1,),
        padding=[(SSM_CONV // 2, SSM_CONV // 2)],
        dimension_numbers=("NWC", "WIO", "NWC"),
        feature_group_count=x.shape[-1])
    return y + b.astype(x.dtype)


def ssd_chunked(x, dt, a_coef, B, C):
    b, s, nh, p = x.shape
    g, n = B.shape[2], B.shape[3]
    hg = nh // g
    L = SSM_CHUNK
    c = s // L
    xs = (x * dt[..., None]).reshape(b, c, L, g, hg, p)
    la_cum = jnp.cumsum((dt * a_coef).reshape(b, c, L, g, hg), axis=2)
    Bc = B.reshape(b, c, L, g, n)
    Cc = C.reshape(b, c, L, g, n)
    lower = jnp.tril(jnp.ones((L, L), dtype=bool))[None, None, :, :, None, None]
    seg = la_cum[:, :, :, None] - la_cum[:, :, None, :]
    decay_ls = jnp.exp(jnp.where(lower, seg, -jnp.inf))
    cb = jnp.einsum('bclgn,bcsgn->bclsg', Cc, Bc)
    y_diag = jnp.einsum('bclsgh,bcsghp->bclghp', cb[..., None] * decay_ls, xs)
    decay_end = jnp.exp(la_cum[:, :, -1:] - la_cum)
    chunk_states = jnp.einsum('bcsgn,bcsghp->bcghpn', Bc, xs * decay_end[..., None])
    chunk_decay = jnp.exp(la_cum[:, :, -1])

    def step(state, inp):
        st, dec = inp
        return state * dec[..., None, None] + st, state

    init = jnp.zeros((b, g, hg, p, n), jnp.float32)
    _, prev = lax.scan(step, init, (jnp.moveaxis(chunk_states, 1, 0), jnp.moveaxis(chunk_decay, 1, 0)))
    prev = jnp.moveaxis(prev, 0, 1)
    y_off = jnp.einsum('bclgn,bcghpn->bclghp', Cc, prev) * jnp.exp(la_cum)[..., None]
    return (y_diag + y_off).reshape(b, s, nh, p)


def apply_rope(t, cos, sin):
    tf = t.astype(jnp.float32)
    t1, t2 = jnp.split(tf, 2, axis=-1)
    out = jnp.concatenate([t1 * cos - t2 * sin, t1 * sin + t2 * cos], axis=-1)
    return out.astype(t.dtype)


def block_attention(q, k, v):
    b, s, nh, dq = q.shape
    nb = s // ATTN_BLOCK
    qb = q.reshape(b, nb, ATTN_BLOCK, nh, dq).transpose(1, 0, 2, 3, 4)
    scale = dq ** -0.5

    def one_block(qblk):
        scores = jnp.einsum('bqhd,bkhd->bhqk', qblk, k).astype(jnp.float32) * scale
        probs = jax.nn.softmax(scores, axis=-1).astype(v.dtype)
        return jnp.einsum('bhqk,bkhd->bqhd', probs, v)

    out = lax.map(one_block, qb)
    return out.transpose(1, 0, 2, 3, 4).reshape(b, s, nh, v.shape[-1])


def setup_inputs(seed: int = 0) -> dict:
    key = jax.random.key(seed)
    ks = iter(jax.random.split(key, 40))
    f32 = jnp.float32
    Ly = DEPTH

    def nrm(shape, fan_in):
        return jax.random.normal(next(ks), shape, f32) * fan_in ** -0.5

    def gain(shape):
        return 1.0 + 0.05 * jax.random.normal(next(ks), shape, f32)

    def dt_bias(shape):
        dt0 = jnp.exp(jax.random.uniform(next(ks), shape, f32, minval=math.log(1e-3), maxval=math.log(1e-1)))
        return dt0 + jnp.log(-jnp.expm1(-dt0))

    def a_log(shape):
        return jnp.log(jax.random.uniform(next(ks), shape, f32, minval=1.0, maxval=16.0))

    x = jax.random.normal(next(ks), (BATCH, SEQ, D_MODEL), f32)
    positions = (jnp.arange(SEQ, dtype=jnp.int32)[None, :]
                 + jax.random.randint(next(ks), (BATCH, 1), 0, 4096, dtype=jnp.int32))
    return {
        "x": x,
        "positions": positions,
        "norm_mix_pre": gain((Ly, D_MODEL)),
        "w_in": nrm((Ly, D_MODEL, D_IN_PROJ), D_MODEL),
        "conv_w": nrm((Ly, SSM_CONV, SSM_CONV_DIM), SSM_CONV),
        "conv_b": 0.02 * jax.random.normal(next(ks), (Ly, SSM_CONV_DIM), f32),
        "dt_bias_fwd": dt_bias((Ly, SSM_N_HEADS)),
        "dt_bias_bwd": dt_bias((Ly, SSM_N_HEADS)),
        "a_log_fwd": a_log((Ly, SSM_N_HEADS)),
        "a_log_bwd": a_log((Ly, SSM_N_HEADS)),
        "d_skip": 1.0 + 0.1 * jax.random.normal(next(ks), (Ly, SSM_N_HEADS), f32),
        "ssm_norm_w": gain((Ly, SSM_D_INNER)),
        "w_ssm_out": nrm((Ly, SSM_D_INNER, D_MODEL), SSM_D_INNER),
        "q_a_norm": gain((Ly, MLA_Q_RANK)),
        "w_q_b": nrm((Ly, MLA_Q_RANK, MLA_N_HEADS * MLA_QK), MLA_Q_RANK),
        "kv_a_norm": gain((Ly, MLA_KV_RANK)),
        "w_kv_b": nrm((Ly, MLA_KV_RANK, MLA_N_HEADS * (MLA_NOPE + MLA_V)), MLA_KV_RANK),
        "w_mla_out": nrm((Ly, MLA_N_HEADS * MLA_V, D_MODEL), MLA_N_HEADS * MLA_V),
        "gate_b": 0.1 * jax.random.normal(next(ks), (Ly, N_BRANCHES * D_MODEL), f32),
        "w_out": nrm((Ly, D_MODEL, D_MODEL), D_MODEL),
        "norm_mix_post": gain((Ly, D_MODEL)),
        "norm_mlp_pre": gain((Ly, D_MODEL)),
        "w_up": nrm((Ly, D_MODEL, D_FF), D_MODEL),
        "w_down": nrm((Ly, D_FF, D_MODEL), D_FF),
        "norm_mlp_post": gain((Ly, D_MODEL)),
    }


def reference(x, positions, norm_mix_pre, w_in, conv_w, conv_b, dt_bias_fwd, dt_bias_bwd,
              a_log_fwd, a_log_bwd, d_skip, ssm_norm_w, w_ssm_out, q_a_norm, w_q_b,
              kv_a_norm, w_kv_b, w_mla_out, gate_b, w_out, norm_mix_post, norm_mlp_pre,
              w_up, w_down, norm_mlp_post):
    b, s, _ = x.shape
    f32 = jnp.float32
    inv_freq = ROPE_BASE ** (-jnp.arange(0, MLA_ROPE, 2, dtype=f32) / MLA_ROPE)
    ang = positions.astype(f32)[..., None] * inv_freq
    cos, sin = jnp.cos(ang), jnp.sin(ang)

    def flip(t):
        return jnp.flip(t, axis=1)

    for l in range(DEPTH):
        h = rms_norm(x, norm_mix_pre[l])
        proj = h @ w_in[l].astype(h.dtype)
        z, xbc, dt_raw, q_lat, kv_lat, k_rope, gate_lin = split_cols(proj, IN_SPLITS)

        xbc = jax.nn.silu(centred_depthwise_conv(xbc, conv_w[l], conv_b[l]))
        xs, Bm, Cm = split_cols(xbc, (SSM_D_INNER, SSM_N_GROUPS * SSM_D_STATE, SSM_N_GROUPS * SSM_D_STATE))
        xh = xs.reshape(b, s, SSM_N_HEADS, SSM_HEAD_DIM).astype(f32)
        Bm = Bm.reshape(b, s, SSM_N_GROUPS, SSM_D_STATE).astype(f32)
        Cm = Cm.reshape(b, s, SSM_N_GROUPS, SSM_D_STATE).astype(f32)
        dt_raw = dt_raw.astype(f32)
        dt_f = jax.nn.softplus(dt_raw[..., :SSM_N_HEADS] + dt_bias_fwd[l].astype(f32))
        dt_b = jax.nn.softplus(dt_raw[..., SSM_N_HEADS:] + dt_bias_bwd[l].astype(f32))
        a_f = -jnp.exp(a_log_fwd[l].astype(f32))
        a_b = -jnp.exp(a_log_bwd[l].astype(f32))
        y_f = ssd_chunked(xh, dt_f, a_f, Bm, Cm)
        y_b = flip(ssd_chunked(flip(xh), flip(dt_b), a_b, flip(Bm), flip(Cm)))
        y = y_f + y_b + d_skip[l].astype(f32)[:, None] * xh
        y = y.reshape(b, s, SSM_D_INNER) * jax.nn.silu(z.astype(f32))
        yg = y.reshape(b, s, SSM_N_GROUPS, SSM_D_INNER // SSM_N_GROUPS)
        yg = yg * lax.rsqrt(jnp.mean(yg * yg, axis=-1, keepdims=True) + EPS)
        y = (yg.reshape(b, s, SSM_D_INNER) * ssm_norm_w[l].astype(f32)).astype(x.dtype)
        y_ssm = y @ w_ssm_out[l].astype(y.dtype)

        q = (rms_norm(q_lat, q_a_norm[l]) @ w_q_b[l].astype(h.dtype)).reshape(b, s, MLA_N_HEADS, MLA_QK)
        q_nope, q_pe = jnp.split(q, [MLA_NOPE], axis=-1)
        kv = (rms_norm(kv_lat, kv_a_norm[l]) @ w_kv_b[l].astype(h.dtype)).reshape(b, s, MLA_N_HEADS, MLA_NOPE + MLA_V)
        k_nope, v = jnp.split(kv, [MLA_NOPE], axis=-1)
        q_pe = apply_rope(q_pe, cos[:, :, None, :], sin[:, :, None, :])
        k_pe = apply_rope(k_rope, cos, sin)
        q_full = jnp.concatenate([q_nope, q_pe], axis=-1)
        k_full = jnp.concatenate([k_nope, jnp.broadcast_to(k_pe[:, :, None, :], (b, s, MLA_N_HEADS, MLA_ROPE))], axis=-1)
        attn = block_attention(q_full, k_full, v).reshape(b, s, MLA_N_HEADS * MLA_V)
        y_mla = attn @ w_mla_out[l].astype(attn.dtype)

        gates = jax.nn.sigmoid(gate_lin + gate_b[l].astype(gate_lin.dtype))
        g_ssm, g_mla = jnp.split(gates, N_BRANCHES, axis=-1)
        mixed = (g_ssm * y_ssm + g_mla * y_mla) @ w_out[l].astype(x.dtype)
        x = x + rms_norm(mixed, norm_mix_post[l])

        hm = rms_norm(x, norm_mlp_pre[l])
        ff = jnp.square(jax.nn.relu(hm @ w_up[l].astype(hm.dtype))) @ w_down[l].astype(hm.dtype)
        x = x + rms_norm(ff, norm_mlp_post[l])
    return x
```

```python
import functools
import math

import numpy as np
import jax
import jax.numpy as jnp
from jax import lax
from jax.experimental import pallas as pl
from jax.experimental.pallas import tpu as pltpu

F32 = jnp.float32
BF16 = jnp.bfloat16

D_MODEL = 1024
SSM_D_INNER = 2048
SSM_HEAD_DIM = 64
SSM_N_HEADS = 32
SSM_N_GROUPS = 8
SSM_D_STATE = 128
SSM_CONV = 5
CHUNK = 128
SSM_CONV_DIM = SSM_D_INNER + 2 * SSM_N_GROUPS * SSM_D_STATE
GROUP_W = SSM_D_INNER // SSM_N_GROUPS
HEADS_PER_GROUP = SSM_N_HEADS // SSM_N_GROUPS
MLA_N_HEADS = 16
MLA_Q_RANK = 256
MLA_KV_RANK = 256
MLA_NOPE = 64
MLA_ROPE = 32
MLA_V = 64
MLA_QK = MLA_NOPE + MLA_ROPE
ROPE_BASE = 10000.0
D_FF = 4 * D_MODEL
EPS = 1e-6
LANES = 128
HEAD_PAD = 128
NEG_BIG = -1e30

BIG_Z = 0
BIG_XBC = SSM_D_INNER
BIG_GATE = SSM_D_INNER + SSM_CONV_DIM
BIG_W = BIG_GATE + 2 * D_MODEL
SM_Q = 0
SM_KV = 256
SM_KRA = 512
SM_KRB = 640
SM_DT = 768
SM_W = 896

VMEM_LIMIT = 52 * 1024 * 1024


def _cparams(sem):
    return pltpu.CompilerParams(dimension_semantics=sem, vmem_limit_bytes=VMEM_LIMIT)


def _norm_matmul_kernel(x_ref, g_ref, w_ref, o_ref, h_ref):
    @pl.when(pl.program_id(1) == 0)
    def _():
        x = x_ref[...]
        ms = jnp.mean(x * x, axis=-1, keepdims=True)
        h_ref[...] = (x * lax.rsqrt(ms + EPS) * g_ref[...]).astype(h_ref.dtype)

    o_ref[...] = jnp.dot(h_ref[...], w_ref[...],
                         preferred_element_type=F32).astype(o_ref.dtype)


def _norm_matmul(x, gain, w, out_dtype, tm, tn):
    t, d = x.shape
    n = w.shape[1]
    return pl.pallas_call(
        _norm_matmul_kernel,
        grid=(t // tm, n // tn),
        in_specs=[
            pl.BlockSpec((tm, d), lambda i, j: (i, 0)),
            pl.BlockSpec((1, d), lambda i, j: (0, 0)),
            pl.BlockSpec((d, tn), lambda i, j: (0, j)),
        ],
        out_specs=pl.BlockSpec((tm, tn), lambda i, j: (i, j)),
        out_shape=jax.ShapeDtypeStruct((t, n), out_dtype),
        scratch_shapes=[pltpu.VMEM((tm, d), BF16)],
        compiler_params=_cparams(("parallel", "arbitrary")),
        name="in_proj",
    )(x, gain, w)


def _dt_prep_kernel(raw_ref, bias_ref, alog_ref, dtc_ref, cumc_ref, dtr_ref, cumr_ref):
    tb = raw_ref.shape[0]
    v = raw_ref[...] + bias_ref[...]
    dt = jnp.maximum(v, 0.0) + jnp.log(1.0 + jnp.exp(-jnp.abs(v)))
    la = dt * (-jnp.exp(alog_ref[...]))
    lane = lax.broadcasted_iota(jnp.int32, (CHUNK, LANES), 1)
    is_fwd = (lane % (2 * HEADS_PER_GROUP)) < HEADS_PER_GROUP
    row = lax.broadcasted_iota(jnp.int32, (CHUNK, CHUNK), 0)
    col = lax.broadcasted_iota(jnp.int32, (CHUNK, CHUNK), 1)
    lower = (col <= row).astype(F32)
    upper = (col >= row).astype(F32)
    cums = []
    for c in range(tb // CHUNK):
        la_c = la[c * CHUNK:(c + 1) * CHUNK]
        la_f = jnp.where(is_fwd, la_c, 0.0)
        la_b = jnp.where(is_fwd, 0.0, la_c)
        cums.append(jnp.dot(lower, la_f, precision=lax.Precision.HIGHEST,
                            preferred_element_type=F32)
                    + jnp.dot(upper, la_b, precision=lax.Precision.HIGHEST,
                              preferred_element_type=F32))
    cum = jnp.concatenate(cums, axis=0)
    dt_t = dt.T
    cum_t = cum.T
    w = 2 * HEADS_PER_GROUP
    for g in range(SSM_N_GROUPS):
        dtc_ref[g] = dt[:, g * w:(g + 1) * w]
        cumc_ref[g] = cum[:, g * w:(g + 1) * w]
        dtr_ref[g] = dt_t[g * w:(g + 1) * w, :]
        cumr_ref[g] = cum_t[g * w:(g + 1) * w, :]


def _dt_prep(small, bias, alog, tb):
    t = small.shape[0]
    w = 2 * HEADS_PER_GROUP
    col_shape = jax.ShapeDtypeStruct((SSM_N_GROUPS, t, w), F32)
    row_shape = jax.ShapeDtypeStruct((SSM_N_GROUPS, w, t), F32)
    return pl.pallas_call(
        _dt_prep_kernel,
        grid=(t // tb,),
        in_specs=[
            pl.BlockSpec((tb, LANES), lambda i: (i, SM_DT // LANES)),
            pl.BlockSpec((1, LANES), lambda i: (0, 0)),
            pl.BlockSpec((1, LANES), lambda i: (0, 0)),
        ],
        out_specs=[
            pl.BlockSpec((SSM_N_GROUPS, tb, w), lambda i: (0, i, 0)),
            pl.BlockSpec((SSM_N_GROUPS, tb, w), lambda i: (0, i, 0)),
            pl.BlockSpec((SSM_N_GROUPS, w, tb), lambda i: (0, 0, i)),
            pl.BlockSpec((SSM_N_GROUPS, w, tb), lambda i: (0, 0, i)),
        ],
        out_shape=[col_shape, col_shape, row_shape, row_shape],
        compiler_params=_cparams(("parallel",)),
        name="dt_prep",
    )(small, bias, alog)


HALO = 16


def _conv_kernel(x_ref, hp_ref, hn_ref, w_ref, b_ref, o_ref):
    si = pl.program_id(1)
    ns = pl.num_programs(1)
    ts = x_ref.shape[1]
    cur = x_ref[0].astype(F32)
    hp = hp_ref[0].astype(F32)[HALO - 8:HALO]
    hn = hn_ref[0].astype(F32)[0:8]
    hp = jnp.where(si > 0, hp, 0.0)
    hn = jnp.where(si < ns - 1, hn, 0.0)
    ext = jnp.concatenate([hp, cur, hn], axis=0)
    n_ext = ts + 16
    w = w_ref[...]
    acc = b_ref[...] + w[2:3, :] * cur
    for k in (0, 1, 3, 4):
        shifted = pltpu.roll(ext, (2 - k) % n_ext, axis=0)[8:8 + ts]
        acc = acc + w[k:k + 1, :] * shifted
    o_ref[0] = (acc * (1.0 / (1.0 + jnp.exp(-acc)))).astype(o_ref.dtype)


def _conv(big, conv_w, conv_b, ts, tc):
    b, s, _ = big.shape
    c_total = conv_w.shape[1]
    col0 = BIG_XBC // tc
    nh = ts // HALO
    last_h = s // HALO - 1
    return pl.pallas_call(
        _conv_kernel,
        grid=(b, s // ts, c_total // tc),
        in_specs=[
            pl.BlockSpec((1, ts, tc), lambda bi, si, ci: (bi, si, col0 + ci)),
            pl.BlockSpec((1, HALO, tc),
                         lambda bi, si, ci: (bi, jnp.maximum(si * nh - 1, 0), col0 + ci)),
            pl.BlockSpec((1, HALO, tc),
                         lambda bi, si, ci: (bi, jnp.minimum((si + 1) * nh, last_h), col0 + ci)),
            pl.BlockSpec((SSM_CONV, tc), lambda bi, si, ci: (0, ci)),
            pl.BlockSpec((1, tc), lambda bi, si, ci: (0, ci)),
        ],
        out_specs=pl.BlockSpec((1, ts, tc), lambda bi, si, ci: (bi, si, ci)),
        out_shape=jax.ShapeDtypeStruct((b, s, c_total), BF16),
        compiler_params=_cparams(("parallel", "parallel", "parallel")),
        name="conv_silu",
    )(big, big, big, conv_w, conv_b)


def _expand_heads(cols, j0):
    rows = cols.shape[0]
    lane = lax.broadcasted_iota(jnp.int32, (rows, GROUP_W), 1)
    out = jnp.broadcast_to(cols[:, j0 + HEADS_PER_GROUP - 1:j0 + HEADS_PER_GROUP], (rows, GROUP_W))
    for j in range(HEADS_PER_GROUP - 2, -1, -1):
        out = jnp.where(lane < (j + 1) * SSM_HEAD_DIM,
                        jnp.broadcast_to(cols[:, j0 + j:j0 + j + 1], (rows, GROUP_W)), out)
    return out


def _select_heads(parts):
    rows = parts[0].shape[0]
    lane = lax.broadcasted_iota(jnp.int32, (rows, GROUP_W), 1)
    out = parts[-1]
    for j in range(len(parts) - 2, -1, -1):
        out = jnp.where(lane < (j + 1) * SSM_HEAD_DIM, parts[j], out)
    return out


def _ssd_kernel(x_ref, b_ref, c_ref, z_ref, dtc_ref, cumc_ref, dtr_ref, cumr_ref,
                dskip_ref, nw_ref, o_ref, sprev_ref, state_ref):
    phase = pl.program_id(2)
    blk = pl.program_id(3)
    nblk = pl.num_programs(3)
    tb = x_ref.shape[1]
    ncb = tb // CHUNK
    H = HEADS_PER_GROUP

    @pl.when(blk == 0)
    def _():
        state_ref[...] = jnp.zeros_like(state_ref)

    @pl.when(phase == 0)
    def _():
        def body(i, carry):
            r0 = pl.multiple_of(i * CHUNK, CHUNK)
            x = x_ref[0, pl.ds(r0, CHUNK), :].astype(F32)
            bm = b_ref[0, pl.ds(r0, CHUNK), :]
            dtc = dtc_ref[0, pl.ds(r0, CHUNK), :]
            cumc = cumc_ref[0, pl.ds(r0, CHUNK), :]
            last = cumc[CHUNK - 1:CHUNK, :]
            wa = dtc * jnp.exp(last - cumc)
            xa = (x * _expand_heads(wa, 0)).astype(BF16)
            s_chunk = lax.dot_general(bm, xa, (((0,), (0,)), ((), ())),
                                      preferred_element_type=F32)
            dec = _expand_heads(jnp.exp(last), 0)
            st = state_ref[...]
            sprev_ref[blk * ncb + i] = st
            state_ref[...] = st * dec + s_chunk
            return carry

        lax.fori_loop(0, ncb, body, 0)

    @pl.when(phase == 1)
    def _():
        dskip = dskip_ref[0]
        nw = nw_ref[0]
        li = lax.broadcasted_iota(jnp.int32, (CHUNK, CHUNK), 0)
        si = lax.broadcasted_iota(jnp.int32, (CHUNK, CHUNK), 1)
        lower = li >= si
        upper = li <= si

        def body(ii, carry):
            i = ncb - 1 - ii
            r0 = pl.multiple_of(i * CHUNK, CHUNK)
            x = x_ref[0, pl.ds(r0, CHUNK), :].astype(F32)
            xb16 = x.astype(BF16)
            bm = b_ref[0, pl.ds(r0, CHUNK), :]
            cm = c_ref[0, pl.ds(r0, CHUNK), :]
            z = z_ref[0, pl.ds(r0, CHUNK), :].astype(F32)
            dtc = dtc_ref[0, pl.ds(r0, CHUNK), :]
            cumc = cumc_ref[0, pl.ds(r0, CHUNK), :]
            dtr = dtr_ref[0, :, pl.ds(r0, CHUNK)]
            cumr = cumr_ref[0, :, pl.ds(r0, CHUNK)]

            cb = lax.dot_general(cm, bm, (((1,), (1,)), ((), ())), preferred_element_type=F32)
            parts = []
            for j in range(H):
                ef = jnp.exp(jnp.where(lower, cumc[:, j:j + 1] - cumr[j:j + 1, :], NEG_BIG))
                eb = jnp.exp(jnp.where(upper, cumc[:, H + j:H + j + 1] - cumr[H + j:H + j + 1, :],
                                       NEG_BIG))
                m = cb * (ef * dtr[j:j + 1, :] + eb * dtr[H + j:H + j + 1, :])
                parts.append(jnp.dot(m.astype(BF16), xb16, preferred_element_type=F32))
            y = _select_heads(parts)

            chunk_id = (nblk - 1 - blk) * ncb + i
            sp_f = sprev_ref[chunk_id].astype(BF16)
            st_b = state_ref[...]
            y = y + jnp.dot(cm, sp_f, preferred_element_type=F32) * _expand_heads(jnp.exp(cumc), 0)
            y = y + (jnp.dot(cm, st_b.astype(BF16), preferred_element_type=F32)
                     * _expand_heads(jnp.exp(cumc), H))
            y = y + dskip * x

            first = cumc[0:1, :]
            wb = dtc * jnp.exp(first - cumc)
            xw = (x * _expand_heads(wb, H)).astype(BF16)
            s_chunk = lax.dot_general(bm, xw, (((0,), (0,)), ((), ())),
                                      preferred_element_type=F32)
            state_ref[...] = st_b * _expand_heads(jnp.exp(first), H) + s_chunk

            yg = y * (z * (1.0 / (1.0 + jnp.exp(-z))))
            ms = jnp.mean(yg * yg, axis=-1, keepdims=True)
            o_ref[0, pl.ds(r0, CHUNK), :] = (yg * lax.rsqrt(ms + EPS) * nw).astype(o_ref.dtype)
            return carry

        lax.fori_loop(0, ncb, body, 0)


def _ssd(conv, big, dtc, cumc, dtr, cumr, dskip, nw, tb):
    b, s, _ = conv.shape
    nblk = s // tb
    w = 2 * HEADS_PER_GROUP
    xw = GROUP_W // SSM_D_STATE

    def sidx(p, k):
        return jnp.where(p == 0, k, nblk - 1 - k)

    def sidx_late(p, k):
        return jnp.where(p == 0, nblk - 1, nblk - 1 - k)

    b_col0 = SSM_D_INNER // SSM_D_STATE
    c_col0 = b_col0 + SSM_N_GROUPS
    return pl.pallas_call(
        _ssd_kernel,
        grid=(b, SSM_N_GROUPS, 2, nblk),
        in_specs=[
            pl.BlockSpec((1, tb, GROUP_W), lambda bi, g, p, k: (bi, sidx(p, k), g)),
            pl.BlockSpec((1, tb, SSM_D_STATE), lambda bi, g, p, k: (bi, sidx(p, k), b_col0 + g)),
            pl.BlockSpec((1, tb, SSM_D_STATE),
                         lambda bi, g, p, k: (bi, sidx_late(p, k), c_col0 + g)),
            pl.BlockSpec((1, tb, GROUP_W), lambda bi, g, p, k: (bi, sidx_late(p, k), g)),
            pl.BlockSpec((1, tb, w), lambda bi, g, p, k: (g, bi * nblk + sidx(p, k), 0)),
            pl.BlockSpec((1, tb, w), lambda bi, g, p, k: (g, bi * nblk + sidx(p, k), 0)),
            pl.BlockSpec((1, w, tb), lambda bi, g, p, k: (g, 0, bi * nblk + sidx_late(p, k))),
            pl.BlockSpec((1, w, tb), lambda bi, g, p, k: (g, 0, bi * nblk + sidx_late(p, k))),
            pl.BlockSpec((1, 1, GROUP_W), lambda bi, g, p, k: (g, 0, 0)),
            pl.BlockSpec((1, 1, GROUP_W), lambda bi, g, p, k: (g, 0, 0)),
        ],
        out_specs=pl.BlockSpec((1, tb, GROUP_W), lambda bi, g, p, k: (bi, sidx_late(p, k), g)),
        out_shape=jax.ShapeDtypeStruct((b, s, SSM_D_INNER), BF16),
        scratch_shapes=[
            pltpu.VMEM((s // CHUNK, SSM_D_STATE, GROUP_W), F32),
            pltpu.VMEM((SSM_D_STATE, GROUP_W), F32),
        ],
        compiler_params=_cparams(("parallel", "parallel", "arbitrary", "arbitrary")),
        name="ssd",
    )(conv, conv, conv, big, dtc, cumc, dtr, cumr, dskip, nw)


def _rms(x, g):
    ms = jnp.mean(x * x, axis=-1, keepdims=True)
    return x * lax.rsqrt(ms + EPS) * g


def _mla_prep_kernel(ql_ref, kvl_ref, kra_ref, krb_ref, pos_ref, invf_ref, sign_ref,
                     gq_ref, gkv_ref, wq_ref, wqs_ref, wk_ref, wv_ref, ones_ref,
                     q_ref, k_ref, v_ref):
    ang = pos_ref[...] * invf_ref[...]
    cos = jnp.cos(ang)
    sin = jnp.sin(ang) * sign_ref[...]
    scale = MLA_QK ** -0.5
    cq = cos * scale
    sq = sin * scale
    qn = _rms(ql_ref[...], gq_ref[...]).astype(BF16)
    qa = jnp.dot(qn, wq_ref[...], preferred_element_type=F32)
    qb = jnp.dot(qn, wqs_ref[...], preferred_element_type=F32)
    kn = _rms(kvl_ref[...], gkv_ref[...]).astype(BF16)
    kk = jnp.dot(kn, wk_ref[...], preferred_element_type=F32)
    vv = jnp.dot(kn, wv_ref[...], preferred_element_type=F32)
    kpe = kra_ref[...] * cos + krb_ref[...] * sin
    for h in range(MLA_N_HEADS):
        sl = slice(h * HEAD_PAD, (h + 1) * HEAD_PAD)
        q_ref[0, h] = (qa[:, sl] * cq + qb[:, sl] * sq).astype(q_ref.dtype)
        k_ref[0, h] = (kk[:, sl] + kpe).astype(k_ref.dtype)
        v_ref[0, h] = (vv[:, sl] + ones_ref[h % 2:h % 2 + 1, :]).astype(v_ref.dtype)


def _mla_prep(small, pos, invf, sign, gq, gkv, wq, wqs, wk, wv, ones, b, s, tm):
    t = small.shape[0]
    nsb = s // tm
    hw = MLA_N_HEADS * HEAD_PAD
    out = jax.ShapeDtypeStruct((b, MLA_N_HEADS, s, HEAD_PAD), BF16)
    const = lambda i: (0, 0)
    ospec = pl.BlockSpec((1, MLA_N_HEADS, tm, HEAD_PAD), lambda i: (i // nsb, 0, i % nsb, 0))
    return pl.pallas_call(
        _mla_prep_kernel,
        grid=(t // tm,),
        in_specs=[
            pl.BlockSpec((tm, MLA_Q_RANK), lambda i: (i, SM_Q // MLA_Q_RANK)),
            pl.BlockSpec((tm, MLA_KV_RANK), lambda i: (i, SM_KV // MLA_KV_RANK)),
            pl.BlockSpec((tm, LANES), lambda i: (i, SM_KRA // LANES)),
            pl.BlockSpec((tm, LANES), lambda i: (i, SM_KRB // LANES)),
            pl.BlockSpec((tm, 1), lambda i: (i, 0)),
            pl.BlockSpec((1, LANES), const),
            pl.BlockSpec((1, LANES), const),
            pl.BlockSpec((1, MLA_Q_RANK), const),
            pl.BlockSpec((1, MLA_KV_RANK), const),
            pl.BlockSpec((MLA_Q_RANK, hw), const),
            pl.BlockSpec((MLA_Q_RANK, hw), const),
            pl.BlockSpec((MLA_KV_RANK, hw), const),
            pl.BlockSpec((MLA_KV_RANK, hw), const),
            pl.BlockSpec((2, LANES), const),
        ],
        out_specs=[ospec, ospec, ospec],
        out_shape=[out, out, out],
        compiler_params=_cparams(("parallel",)),
        name="mla_prep",
    )(small, small, small, small, pos, invf, sign, gq, gkv, wq, wqs, wk, wv, ones)


def _attn_kernel(q_ref, k_ref, v_ref, o_ref, *, tk):
    tq = q_ref.shape[2]
    s = k_ref.shape[2]
    outs = []
    for hh in range(2):
        q = q_ref[0, hh]

        def step(i, carry, hh=hh, q=q):
            m, acc = carry
            r0 = pl.multiple_of(i * tk, tk)
            kt = k_ref[0, hh, pl.ds(r0, tk), :]
            vt = v_ref[0, hh, pl.ds(r0, tk), :]
            sc = lax.dot_general(q, kt, (((1,), (1,)), ((), ())), preferred_element_type=F32)
            m_new = jnp.maximum(m, jnp.max(sc, axis=1, keepdims=True))
            p = jnp.exp(sc - m_new)
            alpha = jnp.exp(m - m_new)
            acc = alpha * acc + jnp.dot(p.astype(BF16), vt, preferred_element_type=F32)
            return m_new, acc

        m0 = jnp.full((tq, 1), NEG_BIG, F32)
        acc0 = jnp.zeros((tq, HEAD_PAD), F32)
        _, acc = lax.fori_loop(0, s // tk, step, (m0, acc0))
        den_lane = MLA_V if hh == 0 else 0
        outs.append(acc / acc[:, den_lane:den_lane + 1])
    lane = lax.broadcasted_iota(jnp.int32, (tq, HEAD_PAD), 1)
    o_ref[0] = jnp.where(lane < MLA_V, outs[0], outs[1]).astype(o_ref.dtype)


def _attention(q, k, v, tq, tk):
    b, h, s, _ = q.shape
    return pl.pallas_call(
        functools.partial(_attn_kernel, tk=tk),
        grid=(b, h // 2, s // tq),
        in_specs=[
            pl.BlockSpec((1, 2, tq, HEAD_PAD), lambda bi, hp, qi: (bi, hp, qi, 0)),
            pl.BlockSpec((1, 2, s, HEAD_PAD), lambda bi, hp, qi: (bi, hp, 0, 0)),
            pl.BlockSpec((1, 2, s, HEAD_PAD), lambda bi, hp, qi: (bi, hp, 0, 0)),
        ],
        out_specs=pl.BlockSpec((1, tq, HEAD_PAD), lambda bi, hp, qi: (bi, qi, hp)),
        out_shape=jax.ShapeDtypeStruct((b, s, h * MLA_V), BF16),
        compiler_params=_cparams(("parallel", "parallel", "arbitrary")),
        name="attention",
    )(q, k, v)


def _merge_kernel(x_ref, y_ref, a_ref, gs_ref, gm_ref, gb_ref, wso_ref, wmo_ref, wo_ref,
                  nw_ref, o_ref):
    y_ssm = jnp.dot(y_ref[...], wso_ref[...], preferred_element_type=F32)
    y_mla = jnp.dot(a_ref[...], wmo_ref[...], preferred_element_type=F32)
    gb = gb_ref[...]
    g_ssm = 1.0 / (1.0 + jnp.exp(-(gs_ref[...].astype(F32) + gb[:, :D_MODEL])))
    g_mla = 1.0 / (1.0 + jnp.exp(-(gm_ref[...].astype(F32) + gb[:, D_MODEL:])))
    mix = (g_ssm * y_ssm + g_mla * y_mla).astype(BF16)
    mixed = jnp.dot(mix, wo_ref[...], preferred_element_type=F32)
    o_ref[...] = x_ref[...] + _rms(mixed, nw_ref[...])


def _merge(x, y, attn, big, gate_b, wso, wmo, wo, nw, tm):
    t, d = x.shape
    const = lambda i: (0, 0)
    g0 = BIG_GATE // D_MODEL
    return pl.pallas_call(
        _merge_kernel,
        grid=(t // tm,),
        in_specs=[
            pl.BlockSpec((tm, d), lambda i: (i, 0)),
            pl.BlockSpec((tm, SSM_D_INNER), lambda i: (i, 0)),
            pl.BlockSpec((tm, d), lambda i: (i, 0)),
            pl.BlockSpec((tm, d), lambda i: (i, g0)),
            pl.BlockSpec((tm, d), lambda i: (i, g0 + 1)),
            pl.BlockSpec((1, 2 * d), const),
            pl.BlockSpec((SSM_D_INNER, d), const),
            pl.BlockSpec((d, d), const),
            pl.BlockSpec((d, d), const),
            pl.BlockSpec((1, d), const),
        ],
        out_specs=pl.BlockSpec((tm, d), lambda i: (i, 0)),
        out_shape=jax.ShapeDtypeStruct((t, d), F32),
        compiler_params=_cparams(("parallel",)),
        name="merge",
    )(x, y, attn, big, big, gate_b, wso, wmo, wo, nw)


def _mlp_kernel(x_ref, g1_ref, wu_ref, wd_ref, g2_ref, o_ref, h_ref, acc_ref):
    j = pl.program_id(1)

    @pl.when(j == 0)
    def _():
        h_ref[...] = _rms(x_ref[...], g1_ref[...]).astype(h_ref.dtype)
        acc_ref[...] = jnp.zeros_like(acc_ref)

    u = jnp.dot(h_ref[...], wu_ref[...], preferred_element_type=F32)
    r = jnp.maximum(u, 0.0)
    acc_ref[...] += jnp.dot((r * r).astype(BF16), wd_ref[...], preferred_element_type=F32)

    @pl.when(j == pl.num_programs(1) - 1)
    def _():
        o_ref[...] = x_ref[...] + _rms(acc_ref[...], g2_ref[...])


def _mlp(x, g1, wu, wd, g2, tm, tf):
    t, d = x.shape
    f = wu.shape[1]
    return pl.pallas_call(
        _mlp_kernel,
        grid=(t // tm, f // tf),
        in_specs=[
            pl.BlockSpec((tm, d), lambda i, j: (i, 0)),
            pl.BlockSpec((1, d), lambda i, j: (0, 0)),
            pl.BlockSpec((d, tf), lambda i, j: (0, j)),
            pl.BlockSpec((tf, d), lambda i, j: (j, 0)),
            pl.BlockSpec((1, d), lambda i, j: (0, 0)),
        ],
        out_specs=pl.BlockSpec((tm, d), lambda i, j: (i, 0)),
        out_shape=jax.ShapeDtypeStruct((t, d), F32),
        scratch_shapes=[pltpu.VMEM((tm, d), BF16), pltpu.VMEM((tm, d), F32)],
        compiler_params=_cparams(("parallel", "arbitrary")),
        name="mlp",
    )(x, g1, wu, wd, g2)


def _group_major(fwd, bwd):
    lead = fwd.shape[:-1]
    f = fwd.reshape(lead + (SSM_N_GROUPS, HEADS_PER_GROUP))
    bw = bwd.reshape(lead + (SSM_N_GROUPS, HEADS_PER_GROUP))
    return jnp.concatenate([f, bw], axis=-1).reshape(lead + (2 * SSM_N_HEADS,))


def _pad_cols(w, n):
    return jnp.pad(w, ((0, 0), (0, n - w.shape[1])))


def _pick(n, prefs):
    for p in prefs:
        if n % p == 0:
            return p
    raise ValueError(f"no tile in {prefs} divides {n}")


def kernel(x, positions, norm_mix_pre, w_in, conv_w, conv_b, dt_bias_fwd, dt_bias_bwd, a_log_fwd, a_log_bwd, d_skip, ssm_norm_w, w_ssm_out, q_a_norm, w_q_b, kv_a_norm, w_kv_b, w_mla_out, gate_b, w_out, norm_mix_post, norm_mlp_pre, w_up, w_down, norm_mlp_post):
    b, s, d = x.shape
    t = b * s
    depth = w_in.shape[0]
    assert d == D_MODEL and s % 512 == 0
    half = MLA_ROPE // 2

    inv_freq = ROPE_BASE ** (-np.arange(0, MLA_ROPE, 2, dtype=np.float32) / MLA_ROPE)
    invf = np.zeros((1, LANES), np.float32)
    invf[0, MLA_NOPE:MLA_NOPE + half] = inv_freq
    invf[0, MLA_NOPE + half:MLA_NOPE + MLA_ROPE] = inv_freq
    sign = np.zeros((1, LANES), np.float32)
    sign[0, MLA_NOPE:MLA_NOPE + half] = -1.0
    sign[0, MLA_NOPE + half:MLA_NOPE + MLA_ROPE] = 1.0
    ones = np.zeros((2, LANES), np.float32)
    ones[0, MLA_V] = 1.0
    ones[1, 0] = 1.0
    invf, sign, ones = jnp.asarray(invf), jnp.asarray(sign), jnp.asarray(ones)
    pos = positions.astype(F32).reshape(t, 1)

    xt = x.reshape(t, d)
    for l in range(depth):
        wz, wxbc, wdt, wql, wkvl, wkr, wg = jnp.split(
            w_in[l], np.cumsum([SSM_D_INNER, SSM_CONV_DIM, 2 * SSM_N_HEADS, MLA_Q_RANK,
                                MLA_KV_RANK, MLA_ROPE])[:].tolist(), axis=1)
        w_big = jnp.concatenate([wz, wxbc, wg], axis=1).astype(BF16)
        wdt_gm = _group_major(wdt[:, :SSM_N_HEADS], wdt[:, SSM_N_HEADS:])
        wkr_sw = jnp.concatenate([wkr[:, half:], wkr[:, :half]], axis=1)
        zpad = jnp.zeros((d, MLA_NOPE), F32)
        w_small = jnp.concatenate([
            wql, wkvl,
            _pad_cols(jnp.concatenate([zpad, wkr], axis=1), LANES),
            _pad_cols(jnp.concatenate([zpad, wkr_sw], axis=1), LANES),
            _pad_cols(wdt_gm, LANES)], axis=1).astype(BF16)
        dt_bias = _pad_cols(_group_major(dt_bias_fwd[l], dt_bias_bwd[l])[None, :], LANES)
        a_log = _pad_cols(_group_major(a_log_fwd[l], a_log_bwd[l])[None, :], LANES)
        dskip = jnp.repeat(d_skip[l], SSM_HEAD_DIM).reshape(SSM_N_GROUPS, 1, GROUP_W)
        nw_ssm = ssm_norm_w[l].reshape(SSM_N_GROUPS, 1, GROUP_W)

        wq3 = w_q_b[l].reshape(MLA_Q_RANK, MLA_N_HEADS, MLA_QK)
        q_nope, q_pe = wq3[..., :MLA_NOPE], wq3[..., MLA_NOPE:]
        q_pe_sw = jnp.concatenate([q_pe[..., half:], q_pe[..., :half]], axis=-1)
        hz = jnp.zeros((MLA_Q_RANK, MLA_N_HEADS, HEAD_PAD - MLA_QK), F32)
        wq = jnp.concatenate([q_nope, q_pe, hz], axis=-1).reshape(MLA_Q_RANK, -1).astype(BF16)
        wqs = jnp.concatenate([jnp.zeros_like(q_nope), q_pe_sw, hz],
                              axis=-1).reshape(MLA_Q_RANK, -1).astype(BF16)
        wkv3 = w_kv_b[l].reshape(MLA_KV_RANK, MLA_N_HEADS, MLA_NOPE + MLA_V)
        k_nope, v_w = wkv3[..., :MLA_NOPE], wkv3[..., MLA_NOPE:]
        z64 = jnp.zeros_like(k_nope)
        wk = jnp.concatenate([k_nope, z64], axis=-1).reshape(MLA_KV_RANK, -1).astype(BF16)
        v_even = jnp.concatenate([v_w, z64], axis=-1)
        v_odd = jnp.concatenate([z64, v_w], axis=-1)
        is_even = (jnp.arange(MLA_N_HEADS) % 2 == 0)[None, :, None]
        wv = jnp.where(is_even, v_even, v_odd).reshape(MLA_KV_RANK, -1).astype(BF16)

        tm = _pick(t, (1024, 512))
        gain = norm_mix_pre[l][None, :]
        big = _norm_matmul(xt, gain, w_big, BF16, tm, 1024)
        small = _norm_matmul(xt, gain, w_small, F32, tm, SM_W)

        dtc, cumc, dtr, cumr = _dt_prep(small, dt_bias, a_log, _pick(t, (1024, 512)))
        big3 = big.reshape(b, s, BIG_W)
        conv = _conv(big3, conv_w[l], conv_b[l][None, :], 512, 512)
        y = _ssd(conv, big3, dtc, cumc, dtr, cumr, dskip, nw_ssm, _pick(s, (1024, 512)))

        q, k, v = _mla_prep(small, pos, invf, sign, q_a_norm[l][None, :], kv_a_norm[l][None, :],
                            wq, wqs, wk, wv, ones, b, s, 512)
        attn = _attention(q, k, v, 512, 512)

        x1 = _merge(xt, y.reshape(t, SSM_D_INNER), attn.reshape(t, d), big,
                    gate_b[l][None, :], w_ssm_out[l].astype(BF16), w_mla_out[l].astype(BF16),
                    w_out[l].astype(BF16), norm_mix_post[l][None, :], 512)

        xt = _mlp(x1, norm_mlp_pre[l][None, :], w_up[l].astype(BF16), w_down[l].astype(BF16),
                  norm_mlp_post[l][None, :], tm, 1024)
    return xt.reshape(b, s, d)
```

```python
import functools
import math

import numpy as np
import jax
import jax.numpy as jnp
from jax import lax
from jax.experimental import pallas as pl
from jax.experimental.pallas import tpu as pltpu

F32 = jnp.float32
BF16 = jnp.bfloat16

D_MODEL = 1024
SSM_D_INNER = 2048
SSM_HEAD_DIM = 64
SSM_N_HEADS = 32
SSM_N_GROUPS = 8
SSM_D_STATE = 128
SSM_CONV = 5
CHUNK = 128
SSM_CONV_DIM = SSM_D_INNER + 2 * SSM_N_GROUPS * SSM_D_STATE
GROUP_W = SSM_D_INNER // SSM_N_GROUPS
HEADS_PER_GROUP = SSM_N_HEADS // SSM_N_GROUPS
MLA_N_HEADS = 16
MLA_Q_RANK = 256
MLA_KV_RANK = 256
MLA_NOPE = 64
MLA_ROPE = 32
MLA_V = 64
MLA_QK = MLA_NOPE + MLA_ROPE
ROPE_BASE = 10000.0
D_FF = 4 * D_MODEL
EPS = 1e-6
LANES = 128
HEAD_PAD = 128
VT_ROWS = 80
VT_ONES_ROW = MLA_V
LOG2E = 1.4426950408889634
NEG_BIG = -1e30

BIG_Z = 0
BIG_XBC = SSM_D_INNER
BIG_GATE = SSM_D_INNER + SSM_CONV_DIM
BIG_W = BIG_GATE + 2 * D_MODEL
SM_Q = 0
SM_KV = 256
SM_KRA = 512
SM_KRB = 640
SM_DT = 768
SM_W = 896

VMEM_LIMIT = 52 * 1024 * 1024


def _cparams(sem):
    return pltpu.CompilerParams(dimension_semantics=sem, vmem_limit_bytes=VMEM_LIMIT)


def _norm_matmul_kernel(x_ref, g_ref, w_ref, o_ref, h_ref):
    @pl.when(pl.program_id(1) == 0)
    def _():
        x = x_ref[...]
        ms = jnp.mean(x * x, axis=-1, keepdims=True)
        h_ref[...] = (x * lax.rsqrt(ms + EPS) * g_ref[...]).astype(h_ref.dtype)

    o_ref[...] = jnp.dot(h_ref[...], w_ref[...],
                         preferred_element_type=F32).astype(o_ref.dtype)


def _norm_matmul(x, gain, w, out_dtype, tm, tn):
    t, d = x.shape
    n = w.shape[1]
    return pl.pallas_call(
        _norm_matmul_kernel,
        grid=(t // tm, n // tn),
        in_specs=[
            pl.BlockSpec((tm, d), lambda i, j: (i, 0)),
            pl.BlockSpec((1, d), lambda i, j: (0, 0)),
            pl.BlockSpec((d, tn), lambda i, j: (0, j)),
        ],
        out_specs=pl.BlockSpec((tm, tn), lambda i, j: (i, j)),
        out_shape=jax.ShapeDtypeStruct((t, n), out_dtype),
        scratch_shapes=[pltpu.VMEM((tm, d), BF16)],
        compiler_params=_cparams(("parallel", "arbitrary")),
        name="in_proj",
    )(x, gain, w)


def _dt_prep_kernel(raw_ref, bias_ref, alog_ref, dtc_ref, cumc_ref, dtr_ref, cumr_ref):
    tb = raw_ref.shape[0]
    v = raw_ref[...] + bias_ref[...]
    dt = jnp.maximum(v, 0.0) + jnp.log(1.0 + jnp.exp(-jnp.abs(v)))
    la = dt * (-jnp.exp(alog_ref[...]))
    lane = lax.broadcasted_iota(jnp.int32, (CHUNK, LANES), 1)
    is_fwd = (lane % (2 * HEADS_PER_GROUP)) < HEADS_PER_GROUP
    row = lax.broadcasted_iota(jnp.int32, (CHUNK, CHUNK), 0)
    col = lax.broadcasted_iota(jnp.int32, (CHUNK, CHUNK), 1)
    lower = (col <= row).astype(F32)
    upper = (col >= row).astype(F32)
    cums = []
    for c in range(tb // CHUNK):
        la_c = la[c * CHUNK:(c + 1) * CHUNK]
        la_f = jnp.where(is_fwd, la_c, 0.0)
        la_b = jnp.where(is_fwd, 0.0, la_c)
        cums.append(jnp.dot(lower, la_f, precision=lax.Precision.HIGHEST,
                            preferred_element_type=F32)
                    + jnp.dot(upper, la_b, precision=lax.Precision.HIGHEST,
                              preferred_element_type=F32))
    cum = jnp.concatenate(cums, axis=0)
    dt_t = dt.T
    cum_t = cum.T
    w = 2 * HEADS_PER_GROUP
    for g in range(SSM_N_GROUPS):
        dtc_ref[g] = dt[:, g * w:(g + 1) * w]
        cumc_ref[g] = cum[:, g * w:(g + 1) * w]
        dtr_ref[g] = dt_t[g * w:(g + 1) * w, :]
        cumr_ref[g] = cum_t[g * w:(g + 1) * w, :]


def _dt_prep(small, bias, alog, tb):
    t = small.shape[0]
    w = 2 * HEADS_PER_GROUP
    col_shape = jax.ShapeDtypeStruct((SSM_N_GROUPS, t, w), F32)
    row_shape = jax.ShapeDtypeStruct((SSM_N_GROUPS, w, t), F32)
    return pl.pallas_call(
        _dt_prep_kernel,
        grid=(t // tb,),
        in_specs=[
            pl.BlockSpec((tb, LANES), lambda i: (i, SM_DT // LANES)),
            pl.BlockSpec((1, LANES), lambda i: (0, 0)),
            pl.BlockSpec((1, LANES), lambda i: (0, 0)),
        ],
        out_specs=[
            pl.BlockSpec((SSM_N_GROUPS, tb, w), lambda i: (0, i, 0)),
            pl.BlockSpec((SSM_N_GROUPS, tb, w), lambda i: (0, i, 0)),
            pl.BlockSpec((SSM_N_GROUPS, w, tb), lambda i: (0, 0, i)),
            pl.BlockSpec((SSM_N_GROUPS, w, tb), lambda i: (0, 0, i)),
        ],
        out_shape=[col_shape, col_shape, row_shape, row_shape],
        compiler_params=_cparams(("parallel",)),
        name="dt_prep",
    )(small, bias, alog)


HALO = 16


def _conv_kernel(x_ref, hp_ref, hn_ref, w_ref, b_ref, o_ref):
    si = pl.program_id(1)
    ns = pl.num_programs(1)
    ts = x_ref.shape[1]
    cur = x_ref[0].astype(F32)
    hp = hp_ref[0].astype(F32)[HALO - 8:HALO]
    hn = hn_ref[0].astype(F32)[0:8]
    hp = jnp.where(si > 0, hp, 0.0)
    hn = jnp.where(si < ns - 1, hn, 0.0)
    ext = jnp.concatenate([hp, cur, hn], axis=0)
    n_ext = ts + 16
    w = w_ref[...]
    acc = b_ref[...] + w[2:3, :] * cur
    for k in (0, 1, 3, 4):
        shifted = pltpu.roll(ext, (2 - k) % n_ext, axis=0)[8:8 + ts]
        acc = acc + w[k:k + 1, :] * shifted
    o_ref[0] = (acc * (1.0 / (1.0 + jnp.exp(-acc)))).astype(o_ref.dtype)


def _conv(big, conv_w, conv_b, ts, tc):
    b, s, _ = big.shape
    c_total = conv_w.shape[1]
    col0 = BIG_XBC // tc
    nh = ts // HALO
    last_h = s // HALO - 1
    return pl.pallas_call(
        _conv_kernel,
        grid=(b, s // ts, c_total // tc),
        in_specs=[
            pl.BlockSpec((1, ts, tc), lambda bi, si, ci: (bi, si, col0 + ci)),
            pl.BlockSpec((1, HALO, tc),
                         lambda bi, si, ci: (bi, jnp.maximum(si * nh - 1, 0), col0 + ci)),
            pl.BlockSpec((1, HALO, tc),
                         lambda bi, si, ci: (bi, jnp.minimum((si + 1) * nh, last_h), col0 + ci)),
            pl.BlockSpec((SSM_CONV, tc), lambda bi, si, ci: (0, ci)),
            pl.BlockSpec((1, tc), lambda bi, si, ci: (0, ci)),
        ],
        out_specs=pl.BlockSpec((1, ts, tc), lambda bi, si, ci: (bi, si, ci)),
        out_shape=jax.ShapeDtypeStruct((b, s, c_total), BF16),
        compiler_params=_cparams(("parallel", "parallel", "parallel")),
        name="conv_silu",
    )(big, big, big, conv_w, conv_b)


def _expand_heads(cols, j0):
    rows = cols.shape[0]
    lane = lax.broadcasted_iota(jnp.int32, (rows, GROUP_W), 1)
    out = jnp.broadcast_to(cols[:, j0 + HEADS_PER_GROUP - 1:j0 + HEADS_PER_GROUP], (rows, GROUP_W))
    for j in range(HEADS_PER_GROUP - 2, -1, -1):
        out = jnp.where(lane < (j + 1) * SSM_HEAD_DIM,
                        jnp.broadcast_to(cols[:, j0 + j:j0 + j + 1], (rows, GROUP_W)), out)
    return out


def _select_heads(parts):
    rows = parts[0].shape[0]
    lane = lax.broadcasted_iota(jnp.int32, (rows, GROUP_W), 1)
    out = parts[-1]
    for j in range(len(parts) - 2, -1, -1):
        out = jnp.where(lane < (j + 1) * SSM_HEAD_DIM, parts[j], out)
    return out


def _ssd_kernel(x_ref, b_ref, c_ref, z_ref, dtc_ref, cumc_ref, dtr_ref, cumr_ref,
                dskip_ref, nw_ref, o_ref, sprev_ref, state_ref):
    phase = pl.program_id(2)
    blk = pl.program_id(3)
    nblk = pl.num_programs(3)
    tb = x_ref.shape[1]
    ncb = tb // CHUNK
    H = HEADS_PER_GROUP

    @pl.when(blk == 0)
    def _():
        state_ref[...] = jnp.zeros_like(state_ref)

    @pl.when(phase == 0)
    def _():
        def body(i, carry):
            r0 = pl.multiple_of(i * CHUNK, CHUNK)
            x = x_ref[0, pl.ds(r0, CHUNK), :].astype(F32)
            bm = b_ref[0, pl.ds(r0, CHUNK), :]
            dtc = dtc_ref[0, pl.ds(r0, CHUNK), :]
            cumc = cumc_ref[0, pl.ds(r0, CHUNK), :]
            last = cumc[CHUNK - 1:CHUNK, :]
            wa = dtc * jnp.exp(last - cumc)
            xa = (x * _expand_heads(wa, 0)).astype(BF16)
            s_chunk = lax.dot_general(bm, xa, (((0,), (0,)), ((), ())),
                                      preferred_element_type=F32)
            dec = _expand_heads(jnp.exp(last), 0)
            st = state_ref[...]
            sprev_ref[blk * ncb + i] = st
            state_ref[...] = st * dec + s_chunk
            return carry

        lax.fori_loop(0, ncb, body, 0)

    @pl.when(phase == 1)
    def _():
        dskip = dskip_ref[0]
        nw = nw_ref[0]
        li = lax.broadcasted_iota(jnp.int32, (CHUNK, CHUNK), 0)
        si = lax.broadcasted_iota(jnp.int32, (CHUNK, CHUNK), 1)
        lower = li >= si
        upper = li <= si

        def body(ii, carry):
            i = ncb - 1 - ii
            r0 = pl.multiple_of(i * CHUNK, CHUNK)
            x = x_ref[0, pl.ds(r0, CHUNK), :].astype(F32)
            xb16 = x.astype(BF16)
            bm = b_ref[0, pl.ds(r0, CHUNK), :]
            cm = c_ref[0, pl.ds(r0, CHUNK), :]
            z = z_ref[0, pl.ds(r0, CHUNK), :].astype(F32)
            dtc = dtc_ref[0, pl.ds(r0, CHUNK), :]
            cumc = cumc_ref[0, pl.ds(r0, CHUNK), :]
            dtr = dtr_ref[0, :, pl.ds(r0, CHUNK)]
            cumr = cumr_ref[0, :, pl.ds(r0, CHUNK)]

            cb = lax.dot_general(cm, bm, (((1,), (1,)), ((), ())), preferred_element_type=F32)
            parts = []
            for j in range(H):
                ef = jnp.exp(jnp.where(lower, cumc[:, j:j + 1] - cumr[j:j + 1, :], NEG_BIG))
                eb = jnp.exp(jnp.where(upper, cumc[:, H + j:H + j + 1] - cumr[H + j:H + j + 1, :],
                                       NEG_BIG))
                m = cb * (ef * dtr[j:j + 1, :] + eb * dtr[H + j:H + j + 1, :])
                parts.append(jnp.dot(m.astype(BF16), xb16, preferred_element_type=F32))
            y = _select_heads(parts)

            chunk_id = (nblk - 1 - blk) * ncb + i
            sp_f = sprev_ref[chunk_id].astype(BF16)
            st_b = state_ref[...]
            y = y + jnp.dot(cm, sp_f, preferred_element_type=F32) * _expand_heads(jnp.exp(cumc), 0)
            y = y + (jnp.dot(cm, st_b.astype(BF16), preferred_element_type=F32)
                     * _expand_heads(jnp.exp(cumc), H))
            y = y + dskip * x

            first = cumc[0:1, :]
            wb = dtc * jnp.exp(first - cumc)
            xw = (x * _expand_heads(wb, H)).astype(BF16)
            s_chunk = lax.dot_general(bm, xw, (((0,), (0,)), ((), ())),
                                      preferred_element_type=F32)
            state_ref[...] = st_b * _expand_heads(jnp.exp(first), H) + s_chunk

            yg = y * (z * (1.0 / (1.0 + jnp.exp(-z))))
            ms = jnp.mean(yg * yg, axis=-1, keepdims=True)
            o_ref[0, pl.ds(r0, CHUNK), :] = (yg * lax.rsqrt(ms + EPS) * nw).astype(o_ref.dtype)
            return carry

        lax.fori_loop(0, ncb, body, 0)


def _ssd(conv, big, dtc, cumc, dtr, cumr, dskip, nw, tb):
    b, s, _ = conv.shape
    nblk = s // tb
    w = 2 * HEADS_PER_GROUP
    xw = GROUP_W // SSM_D_STATE

    def sidx(p, k):
        return jnp.where(p == 0, k, nblk - 1 - k)

    def sidx_late(p, k):
        return jnp.where(p == 0, nblk - 1, nblk - 1 - k)

    b_col0 = SSM_D_INNER // SSM_D_STATE
    c_col0 = b_col0 + SSM_N_GROUPS
    return pl.pallas_call(
        _ssd_kernel,
        grid=(b, SSM_N_GROUPS, 2, nblk),
        in_specs=[
            pl.BlockSpec((1, tb, GROUP_W), lambda bi, g, p, k: (bi, sidx(p, k), g)),
            pl.BlockSpec((1, tb, SSM_D_STATE), lambda bi, g, p, k: (bi, sidx(p, k), b_col0 + g)),
            pl.BlockSpec((1, tb, SSM_D_STATE),
                         lambda bi, g, p, k: (bi, sidx_late(p, k), c_col0 + g)),
            pl.BlockSpec((1, tb, GROUP_W), lambda bi, g, p, k: (bi, sidx_late(p, k), g)),
            pl.BlockSpec((1, tb, w), lambda bi, g, p, k: (g, bi * nblk + sidx(p, k), 0)),
            pl.BlockSpec((1, tb, w), lambda bi, g, p, k: (g, bi * nblk + sidx(p, k), 0)),
            pl.BlockSpec((1, w, tb), lambda bi, g, p, k: (g, 0, bi * nblk + sidx_late(p, k))),
            pl.BlockSpec((1, w, tb), lambda bi, g, p, k: (g, 0, bi * nblk + sidx_late(p, k))),
            pl.BlockSpec((1, 1, GROUP_W), lambda bi, g, p, k: (g, 0, 0)),
            pl.BlockSpec((1, 1, GROUP_W), lambda bi, g, p, k: (g, 0, 0)),
        ],
        out_specs=pl.BlockSpec((1, tb, GROUP_W), lambda bi, g, p, k: (bi, sidx_late(p, k), g)),
        out_shape=jax.ShapeDtypeStruct((b, s, SSM_D_INNER), BF16),
        scratch_shapes=[
            pltpu.VMEM((s // CHUNK, SSM_D_STATE, GROUP_W), F32),
            pltpu.VMEM((SSM_D_STATE, GROUP_W), F32),
        ],
        compiler_params=_cparams(("parallel", "parallel", "arbitrary", "arbitrary")),
        name="ssd",
    )(conv, conv, conv, big, dtc, cumc, dtr, cumr, dskip, nw)


def _rms(x, g):
    ms = jnp.mean(x * x, axis=-1, keepdims=True)
    return x * lax.rsqrt(ms + EPS) * g


def _mla_prep_kernel(ql_ref, kvl_ref, kra_ref, krb_ref, pos_ref, invf_ref, sign_ref,
                     gq_ref, gkv_ref, wq_ref, wqs_ref, wk_ref, wvt_ref, ones_ref,
                     q_ref, k_ref, vt_ref):
    ang = pos_ref[...] * invf_ref[...]
    cos = jnp.cos(ang)
    sin = jnp.sin(ang) * sign_ref[...]
    scale = MLA_QK ** -0.5 * LOG2E
    cq = cos * scale
    sq = sin * scale
    qn = _rms(ql_ref[...], gq_ref[...]).astype(BF16)
    qa = jnp.dot(qn, wq_ref[...], preferred_element_type=F32)
    qb = jnp.dot(qn, wqs_ref[...], preferred_element_type=F32)
    kn = _rms(kvl_ref[...], gkv_ref[...]).astype(BF16)
    kk = jnp.dot(kn, wk_ref[...], preferred_element_type=F32)
    vvt = lax.dot_general(wvt_ref[...], kn, (((1,), (1,)), ((), ())),
                          preferred_element_type=F32) + ones_ref[...]
    kpe = kra_ref[...] * cos + krb_ref[...] * sin
    for h in range(MLA_N_HEADS):
        sl = slice(h * HEAD_PAD, (h + 1) * HEAD_PAD)
        q_ref[0, h] = (qa[:, sl] * cq + qb[:, sl] * sq).astype(q_ref.dtype)
        k_ref[0, h] = (kk[:, sl] + kpe).astype(k_ref.dtype)
        vt_ref[0, h] = vvt[h * VT_ROWS:(h + 1) * VT_ROWS, :].astype(vt_ref.dtype)


def _mla_prep(small, pos, invf, sign, gq, gkv, wq, wqs, wk, wvt, ones, b, s, tm):
    t = small.shape[0]
    nsb = s // tm
    hw = MLA_N_HEADS * HEAD_PAD
    out = jax.ShapeDtypeStruct((b, MLA_N_HEADS, s, HEAD_PAD), BF16)
    out_vt = jax.ShapeDtypeStruct((b, MLA_N_HEADS, VT_ROWS, s), BF16)
    const = lambda i: (0, 0)
    ospec = pl.BlockSpec((1, MLA_N_HEADS, tm, HEAD_PAD), lambda i: (i // nsb, 0, i % nsb, 0))
    ospec_vt = pl.BlockSpec((1, MLA_N_HEADS, VT_ROWS, tm), lambda i: (i // nsb, 0, 0, i % nsb))
    return pl.pallas_call(
        _mla_prep_kernel,
        grid=(t // tm,),
        in_specs=[
            pl.BlockSpec((tm, MLA_Q_RANK), lambda i: (i, SM_Q // MLA_Q_RANK)),
            pl.BlockSpec((tm, MLA_KV_RANK), lambda i: (i, SM_KV // MLA_KV_RANK)),
            pl.BlockSpec((tm, LANES), lambda i: (i, SM_KRA // LANES)),
            pl.BlockSpec((tm, LANES), lambda i: (i, SM_KRB // LANES)),
            pl.BlockSpec((tm, 1), lambda i: (i, 0)),
            pl.BlockSpec((1, LANES), const),
            pl.BlockSpec((1, LANES), const),
            pl.BlockSpec((1, MLA_Q_RANK), const),
            pl.BlockSpec((1, MLA_KV_RANK), const),
            pl.BlockSpec((MLA_Q_RANK, hw), const),
            pl.BlockSpec((MLA_Q_RANK, hw), const),
            pl.BlockSpec((MLA_KV_RANK, hw), const),
            pl.BlockSpec((MLA_N_HEADS * VT_ROWS, MLA_KV_RANK), const),
            pl.BlockSpec((MLA_N_HEADS * VT_ROWS, 1), const),
        ],
        out_specs=[ospec, ospec, ospec_vt],
        out_shape=[out, out, out_vt],
        compiler_params=_cparams(("parallel",)),
        name="mla_prep",
    )(small, small, small, small, pos, invf, sign, gq, gkv, wq, wqs, wk, wvt, ones)


def _attn_kernel(q_ref, k_ref, vt_ref, o_ref, s_ref, *, tk):
    tq = q_ref.shape[2]
    s = k_ref.shape[2]
    qs = (q_ref[0, 0], q_ref[0, 1])

    n_kv = s // tk

    def scores(i, slot):
        r0 = pl.multiple_of(i * tk, tk)
        for hh in range(2):
            s_ref[slot, hh] = lax.dot_general(k_ref[0, hh, pl.ds(r0, tk), :], qs[hh],
                                              (((1,), (1,)), ((), ())),
                                              preferred_element_type=F32)

    def consume(i, slot, stats):
        r0 = pl.multiple_of(i * tk, tk)
        new = []
        for hh in range(2):
            m, acc = stats[2 * hh], stats[2 * hh + 1]
            vt = vt_ref[0, hh, :, pl.ds(r0, tk)]
            m_new = jnp.maximum(m, jnp.max(s_ref[slot, hh], axis=0, keepdims=True))
            p = jnp.exp2(s_ref[slot, hh] - m_new).astype(BF16)
            alpha = jnp.exp2(m - m_new)
            acc = acc * alpha + jnp.dot(vt, p, preferred_element_type=F32)
            new += [m_new, acc]
        return tuple(new)

    def step(j, stats):
        scores(2 * j + 1, 1)
        stats = consume(2 * j, 0, stats)
        scores(2 * j + 2, 0)
        return consume(2 * j + 1, 1, stats)

    m0 = jnp.full((1, tq), NEG_BIG, F32)
    acc0 = jnp.zeros((VT_ROWS, tq), F32)
    scores(0, 0)
    stats = lax.fori_loop(0, n_kv // 2 - 1, step, (m0, acc0, m0, acc0))
    scores(n_kv - 1, 1)
    stats = consume(n_kv - 2, 0, stats)
    res = consume(n_kv - 1, 1, stats)
    outs = [res[2 * hh + 1][:MLA_V] / res[2 * hh + 1][VT_ONES_ROW:VT_ONES_ROW + 1]
            for hh in range(2)]
    o_ref[0] = jnp.concatenate(outs, axis=0).T.astype(o_ref.dtype)


def _attention(q, k, vt, tq, tk):
    b, h, s, _ = q.shape
    return pl.pallas_call(
        functools.partial(_attn_kernel, tk=tk),
        grid=(b, h // 2, s // tq),
        in_specs=[
            pl.BlockSpec((1, 2, tq, HEAD_PAD), lambda bi, hp, qi: (bi, hp, qi, 0)),
            pl.BlockSpec((1, 2, s, HEAD_PAD), lambda bi, hp, qi: (bi, hp, 0, 0)),
            pl.BlockSpec((1, 2, VT_ROWS, s), lambda bi, hp, qi: (bi, hp, 0, 0)),
        ],
        out_specs=pl.BlockSpec((1, tq, 2 * MLA_V), lambda bi, hp, qi: (bi, qi, hp)),
        out_shape=jax.ShapeDtypeStruct((b, s, h * MLA_V), BF16),
        scratch_shapes=[pltpu.VMEM((2, 2, tk, tq), F32)],
        compiler_params=_cparams(("parallel", "parallel", "arbitrary")),
        name="attention",
    )(q, k, vt)


def _merge_kernel(x_ref, y_ref, a_ref, gs_ref, gm_ref, gb_ref, wso_ref, wmo_ref, wo_ref,
                  nw_ref, o_ref):
    y_ssm = jnp.dot(y_ref[...], wso_ref[...], preferred_element_type=F32)
    y_mla = jnp.dot(a_ref[...], wmo_ref[...], preferred_element_type=F32)
    gb = gb_ref[...]
    g_ssm = 1.0 / (1.0 + jnp.exp(-(gs_ref[...].astype(F32) + gb[:, :D_MODEL])))
    g_mla = 1.0 / (1.0 + jnp.exp(-(gm_ref[...].astype(F32) + gb[:, D_MODEL:])))
    mix = (g_ssm * y_ssm + g_mla * y_mla).astype(BF16)
    mixed = jnp.dot(mix, wo_ref[...], preferred_element_type=F32)
    o_ref[...] = x_ref[...] + _rms(mixed, nw_ref[...])


def _merge(x, y, attn, big, gate_b, wso, wmo, wo, nw, tm):
    t, d = x.shape
    const = lambda i: (0, 0)
    g0 = BIG_GATE // D_MODEL
    return pl.pallas_call(
        _merge_kernel,
        grid=(t // tm,),
        in_specs=[
            pl.BlockSpec((tm, d), lambda i: (i, 0)),
            pl.BlockSpec((tm, SSM_D_INNER), lambda i: (i, 0)),
            pl.BlockSpec((tm, d), lambda i: (i, 0)),
            pl.BlockSpec((tm, d), lambda i: (i, g0)),
            pl.BlockSpec((tm, d), lambda i: (i, g0 + 1)),
            pl.BlockSpec((1, 2 * d), const),
            pl.BlockSpec((SSM_D_INNER, d), const),
            pl.BlockSpec((d, d), const),
            pl.BlockSpec((d, d), const),
            pl.BlockSpec((1, d), const),
        ],
        out_specs=pl.BlockSpec((tm, d), lambda i: (i, 0)),
        out_shape=jax.ShapeDtypeStruct((t, d), F32),
        compiler_params=_cparams(("parallel",)),
        name="merge",
    )(x, y, attn, big, big, gate_b, wso, wmo, wo, nw)


def _mlp_kernel(x_ref, g1_ref, wu_ref, wd_ref, g2_ref, o_ref, h_ref, acc_ref):
    j = pl.program_id(1)

    @pl.when(j == 0)
    def _():
        h_ref[...] = _rms(x_ref[...], g1_ref[...]).astype(h_ref.dtype)
        acc_ref[...] = jnp.zeros_like(acc_ref)

    u = jnp.dot(h_ref[...], wu_ref[...], preferred_element_type=F32)
    r = jnp.maximum(u, 0.0)
    acc_ref[...] += jnp.dot((r * r).astype(BF16), wd_ref[...], preferred_element_type=F32)

    @pl.when(j == pl.num_programs(1) - 1)
    def _():
        o_ref[...] = x_ref[...] + _rms(acc_ref[...], g2_ref[...])


def _mlp(x, g1, wu, wd, g2, tm, tf):
    t, d = x.shape
    f = wu.shape[1]
    return pl.pallas_call(
        _mlp_kernel,
        grid=(t // tm, f // tf),
        in_specs=[
            pl.BlockSpec((tm, d), lambda i, j: (i, 0)),
            pl.BlockSpec((1, d), lambda i, j: (0, 0)),
            pl.BlockSpec((d, tf), lambda i, j: (0, j)),
            pl.BlockSpec((tf, d), lambda i, j: (j, 0)),
            pl.BlockSpec((1, d), lambda i, j: (0, 0)),
        ],
        out_specs=pl.BlockSpec((tm, d), lambda i, j: (i, 0)),
        out_shape=jax.ShapeDtypeStruct((t, d), F32),
        scratch_shapes=[pltpu.VMEM((tm, d), BF16), pltpu.VMEM((tm, d), F32)],
        compiler_params=_cparams(("parallel", "arbitrary")),
        name="mlp",
    )(x, g1, wu, wd, g2)


def _group_major(fwd, bwd):
    lead = fwd.shape[:-1]
    f = fwd.reshape(lead + (SSM_N_GROUPS, HEADS_PER_GROUP))
    bw = bwd.reshape(lead + (SSM_N_GROUPS, HEADS_PER_GROUP))
    return jnp.concatenate([f, bw], axis=-1).reshape(lead + (2 * SSM_N_HEADS,))


def _pad_cols(w, n):
    return jnp.pad(w, ((0, 0), (0, n - w.shape[1])))


def _pick(n, prefs):
    for p in prefs:
        if n % p == 0:
            return p
    raise ValueError(f"no tile in {prefs} divides {n}")


def kernel(x, positions, norm_mix_pre, w_in, conv_w, conv_b, dt_bias_fwd, dt_bias_bwd, a_log_fwd, a_log_bwd, d_skip, ssm_norm_w, w_ssm_out, q_a_norm, w_q_b, kv_a_norm, w_kv_b, w_mla_out, gate_b, w_out, norm_mix_post, norm_mlp_pre, w_up, w_down, norm_mlp_post):
    b, s, d = x.shape
    t = b * s
    depth = w_in.shape[0]
    assert d == D_MODEL and s % 512 == 0
    half = MLA_ROPE // 2

    inv_freq = ROPE_BASE ** (-np.arange(0, MLA_ROPE, 2, dtype=np.float32) / MLA_ROPE)
    invf = np.zeros((1, LANES), np.float32)
    invf[0, MLA_NOPE:MLA_NOPE + half] = inv_freq
    invf[0, MLA_NOPE + half:MLA_NOPE + MLA_ROPE] = inv_freq
    sign = np.zeros((1, LANES), np.float32)
    sign[0, MLA_NOPE:MLA_NOPE + half] = -1.0
    sign[0, MLA_NOPE + half:MLA_NOPE + MLA_ROPE] = 1.0
    ones = np.zeros((MLA_N_HEADS, VT_ROWS, 1), np.float32)
    ones[:, VT_ONES_ROW] = 1.0
    ones = ones.reshape(MLA_N_HEADS * VT_ROWS, 1)
    invf, sign, ones = jnp.asarray(invf), jnp.asarray(sign), jnp.asarray(ones)
    pos = positions.astype(F32).reshape(t, 1)

    xt = x.reshape(t, d)
    for l in range(depth):
        wz, wxbc, wdt, wql, wkvl, wkr, wg = jnp.split(
            w_in[l], np.cumsum([SSM_D_INNER, SSM_CONV_DIM, 2 * SSM_N_HEADS, MLA_Q_RANK,
                                MLA_KV_RANK, MLA_ROPE])[:].tolist(), axis=1)
        w_big = jnp.concatenate([wz, wxbc, wg], axis=1).astype(BF16)
        wdt_gm = _group_major(wdt[:, :SSM_N_HEADS], wdt[:, SSM_N_HEADS:])
        wkr_sw = jnp.concatenate([wkr[:, half:], wkr[:, :half]], axis=1)
        zpad = jnp.zeros((d, MLA_NOPE), F32)
        w_small = jnp.concatenate([
            wql, wkvl,
            _pad_cols(jnp.concatenate([zpad, wkr], axis=1), LANES),
            _pad_cols(jnp.concatenate([zpad, wkr_sw], axis=1), LANES),
            _pad_cols(wdt_gm, LANES)], axis=1).astype(BF16)
        dt_bias = _pad_cols(_group_major(dt_bias_fwd[l], dt_bias_bwd[l])[None, :], LANES)
        a_log = _pad_cols(_group_major(a_log_fwd[l], a_log_bwd[l])[None, :], LANES)
        dskip = jnp.repeat(d_skip[l], SSM_HEAD_DIM).reshape(SSM_N_GROUPS, 1, GROUP_W)
        nw_ssm = ssm_norm_w[l].reshape(SSM_N_GROUPS, 1, GROUP_W)

        wq3 = w_q_b[l].reshape(MLA_Q_RANK, MLA_N_HEADS, MLA_QK)
        q_nope, q_pe = wq3[..., :MLA_NOPE], wq3[..., MLA_NOPE:]
        q_pe_sw = jnp.concatenate([q_pe[..., half:], q_pe[..., :half]], axis=-1)
        hz = jnp.zeros((MLA_Q_RANK, MLA_N_HEADS, HEAD_PAD - MLA_QK), F32)
        wq = jnp.concatenate([q_nope, q_pe, hz], axis=-1).reshape(MLA_Q_RANK, -1).astype(BF16)
        wqs = jnp.concatenate([jnp.zeros_like(q_nope), q_pe_sw, hz],
                              axis=-1).reshape(MLA_Q_RANK, -1).astype(BF16)
        wkv3 = w_kv_b[l].reshape(MLA_KV_RANK, MLA_N_HEADS, MLA_NOPE + MLA_V)
        k_nope, v_w = wkv3[..., :MLA_NOPE], wkv3[..., MLA_NOPE:]
        z64 = jnp.zeros_like(k_nope)
        wk = jnp.concatenate([k_nope, z64], axis=-1).reshape(MLA_KV_RANK, -1).astype(BF16)
        wvt = jnp.pad(jnp.transpose(v_w, (1, 2, 0)), ((0, 0), (0, VT_ROWS - MLA_V), (0, 0)))
        wvt = wvt.reshape(MLA_N_HEADS * VT_ROWS, MLA_KV_RANK).astype(BF16)

        tm = _pick(t, (1024, 512))
        gain = norm_mix_pre[l][None, :]
        big = _norm_matmul(xt, gain, w_big, BF16, tm, 1024)
        small = _norm_matmul(xt, gain, w_small, F32, tm, SM_W)

        dtc, cumc, dtr, cumr = _dt_prep(small, dt_bias, a_log, _pick(t, (1024, 512)))
        big3 = big.reshape(b, s, BIG_W)
        conv = _conv(big3, conv_w[l], conv_b[l][None, :], 512, 512)
        y = _ssd(conv, big3, dtc, cumc, dtr, cumr, dskip, nw_ssm, _pick(s, (1024, 512)))

        q, k, vt = _mla_prep(small, pos, invf, sign, q_a_norm[l][None, :], kv_a_norm[l][None, :],
                             wq, wqs, wk, wvt, ones, b, s, 512)
        attn = _attention(q, k, vt, 512, 512)

        x1 = _merge(xt, y.reshape(t, SSM_D_INNER), attn.reshape(t, d), big,
                    gate_b[l][None, :], w_ssm_out[l].astype(BF16), w_mla_out[l].astype(BF16),
                    w_out[l].astype(BF16), norm_mix_post[l][None, :], 512)

        xt = _mlp(x1, norm_mlp_pre[l][None, :], w_up[l].astype(BF16), w_down[l].astype(BF16),
                  norm_mlp_post[l][None, :], tm, 1024)
    return xt.reshape(b, s, d)
```

```python
import functools
import math

import numpy as np
import jax
import jax.numpy as jnp
from jax import lax
from jax.experimental import pallas as pl
from jax.experimental.pallas import tpu as pltpu

F32 = jnp.float32
BF16 = jnp.bfloat16

D_MODEL = 1024
SSM_D_INNER = 2048
SSM_HEAD_DIM = 64
SSM_N_HEADS = 32
SSM_N_GROUPS = 8
SSM_D_STATE = 128
SSM_CONV = 5
CHUNK = 128
SSM_CONV_DIM = SSM_D_INNER + 2 * SSM_N_GROUPS * SSM_D_STATE
GROUP_W = SSM_D_INNER // SSM_N_GROUPS
HEADS_PER_GROUP = SSM_N_HEADS // SSM_N_GROUPS
MLA_N_HEADS = 16
MLA_Q_RANK = 256
MLA_KV_RANK = 256
MLA_NOPE = 64
MLA_ROPE = 32
MLA_V = 64
MLA_QK = MLA_NOPE + MLA_ROPE
ROPE_BASE = 10000.0
D_FF = 4 * D_MODEL
EPS = 1e-6
LANES = 128
HEAD_PAD = 128
VT_ROWS = 80
VT_ONES_ROW = MLA_V
LOG2E = 1.4426950408889634
NEG_BIG = -1e30
SCORE_BOUND = 80.0

BIG_Z = 0
BIG_XBC = SSM_D_INNER
BIG_GATE = SSM_D_INNER + SSM_CONV_DIM
BIG_W = BIG_GATE + 2 * D_MODEL
SM_Q = 0
SM_KV = 256
SM_KRA = 512
SM_KRB = 640
SM_DT = 768
SM_W = 896

VMEM_LIMIT = 52 * 1024 * 1024

ATT_TQ = 512
ATT_TK = 512
ATT_UNROLL = 4
SSD_UNROLL_FWD = 8
SSD_UNROLL_BWD = 8


def _cparams(sem):
    return pltpu.CompilerParams(dimension_semantics=sem, vmem_limit_bytes=VMEM_LIMIT)


def _norm_matmul_kernel(x_ref, g_ref, w_ref, o_ref, h_ref):
    @pl.when(pl.program_id(1) == 0)
    def _():
        x = x_ref[...]
        ms = jnp.mean(x * x, axis=-1, keepdims=True)
        h_ref[...] = (x * lax.rsqrt(ms + EPS) * g_ref[...]).astype(h_ref.dtype)

    o_ref[...] = jnp.dot(h_ref[...], w_ref[...],
                         preferred_element_type=F32).astype(o_ref.dtype)


def _norm_matmul(x, gain, w, out_dtype, tm, tn):
    t, d = x.shape
    n = w.shape[1]
    return pl.pallas_call(
        _norm_matmul_kernel,
        grid=(t // tm, n // tn),
        in_specs=[
            pl.BlockSpec((tm, d), lambda i, j: (i, 0)),
            pl.BlockSpec((1, d), lambda i, j: (0, 0)),
            pl.BlockSpec((d, tn), lambda i, j: (0, j)),
        ],
        out_specs=pl.BlockSpec((tm, tn), lambda i, j: (i, j)),
        out_shape=jax.ShapeDtypeStruct((t, n), out_dtype),
        scratch_shapes=[pltpu.VMEM((tm, d), BF16)],
        compiler_params=_cparams(("parallel", "arbitrary")),
        name="in_proj",
    )(x, gain, w)


def _split3(x):
    hi = x.astype(BF16)
    r = x - hi.astype(F32)
    mid = r.astype(BF16)
    lo = (r - mid.astype(F32)).astype(BF16)
    return [hi, mid, lo]


def _dt_prep_kernel(raw_ref, bias_ref, alog_ref, perm_ref, pack_ref, dtr_ref, cumr_ref):
    tb = raw_ref.shape[0]
    v = raw_ref[...] + bias_ref[...]
    dt = jnp.maximum(v, 0.0) + jnp.log(1.0 + jnp.exp(-jnp.abs(v)))
    la = dt * (-jnp.exp(alog_ref[...]))
    lane = lax.broadcasted_iota(jnp.int32, (CHUNK, LANES), 1)
    is_fwd = (lane % (2 * HEADS_PER_GROUP)) < HEADS_PER_GROUP
    row = lax.broadcasted_iota(jnp.int32, (CHUNK, CHUNK), 0)
    col = lax.broadcasted_iota(jnp.int32, (CHUNK, CHUNK), 1)
    lower = (col <= row).astype(F32)
    upper = (col >= row).astype(F32)
    cums, wgts = [], []
    for c in range(tb // CHUNK):
        la_c = la[c * CHUNK:(c + 1) * CHUNK]
        la_f = jnp.where(is_fwd, la_c, 0.0)
        la_b = jnp.where(is_fwd, 0.0, la_c)
        cum_c = (jnp.dot(lower, la_f, precision=lax.Precision.HIGHEST,
                         preferred_element_type=F32)
                 + jnp.dot(upper, la_b, precision=lax.Precision.HIGHEST,
                           preferred_element_type=F32))
        edge = jnp.where(is_fwd[:1], cum_c[CHUNK - 1:CHUNK], cum_c[0:1])
        cums.append(cum_c)
        wgts.append(dt[c * CHUNK:(c + 1) * CHUNK] * jnp.exp(edge - cum_c))
    cum = jnp.concatenate(cums, axis=0)
    wgt = jnp.concatenate(wgts, axis=0)
    parts = _split3(jnp.exp(cum)) + _split3(wgt) + _split3(cum)
    pack_ref[...] = jnp.dot(jnp.concatenate(parts, axis=1), perm_ref[...],
                            preferred_element_type=F32).astype(pack_ref.dtype)
    dt_t = dt.T
    cum_t = cum.T
    w = 2 * HEADS_PER_GROUP
    for g in range(SSM_N_GROUPS):
        dtr_ref[g] = dt_t[g * w:(g + 1) * w, :]
        cumr_ref[g] = cum_t[g * w:(g + 1) * w, :]


def _dt_prep(small, bias, alog, perm, tb):
    t = small.shape[0]
    w = 2 * HEADS_PER_GROUP
    row_shape = jax.ShapeDtypeStruct((SSM_N_GROUPS, w, t), F32)
    return pl.pallas_call(
        _dt_prep_kernel,
        grid=(t // tb,),
        in_specs=[
            pl.BlockSpec((tb, LANES), lambda i: (i, SM_DT // LANES)),
            pl.BlockSpec((1, LANES), lambda i: (0, 0)),
            pl.BlockSpec((1, LANES), lambda i: (0, 0)),
            pl.BlockSpec(perm.shape, lambda i: (0, 0)),
        ],
        out_specs=[
            pl.BlockSpec((tb, SSM_N_GROUPS * LANES), lambda i: (i, 0)),
            pl.BlockSpec((SSM_N_GROUPS, w, tb), lambda i: (0, 0, i)),
            pl.BlockSpec((SSM_N_GROUPS, w, tb), lambda i: (0, 0, i)),
        ],
        out_shape=[jax.ShapeDtypeStruct((t, SSM_N_GROUPS * LANES), BF16), row_shape, row_shape],
        compiler_params=_cparams(("parallel",)),
        name="dt_prep",
    )(small, bias, alog, perm)


def _pack_constants():
    w = 2 * HEADS_PER_GROUP
    n_parts = 9
    perm = np.zeros((n_parts * LANES, SSM_N_GROUPS * LANES), np.float32)
    for q in range(n_parts):
        for g in range(SSM_N_GROUPS):
            for i in range(w):
                perm[q * LANES + g * w + i, g * LANES + q * w + i] = 1.0

    def expand_cols(first_part, j0):
        m = np.zeros((LANES, GROUP_W), np.float32)
        for p in range(GROUP_W):
            for q in range(3):
                m[(first_part + q) * w + j0 + p // SSM_HEAD_DIM, p] = 1.0
        return m

    def bcast_cols(first_part):
        m = np.zeros((LANES, w * CHUNK), np.float32)
        for i in range(w):
            for q in range(3):
                m[(first_part + q) * w + i, i * CHUNK:(i + 1) * CHUNK] = 1.0
        return m

    h = HEADS_PER_GROUP
    sel0 = np.concatenate([expand_cols(3, 0), expand_cols(0, 0)], axis=1)
    sel1 = np.concatenate([expand_cols(0, 0), expand_cols(0, h), expand_cols(3, h),
                           bcast_cols(6)], axis=1)
    return perm, sel0, sel1


HALO = 16


def _conv_kernel(x_ref, hp_ref, hn_ref, w_ref, b_ref, o_ref):
    si = pl.program_id(1)
    ns = pl.num_programs(1)
    ts = x_ref.shape[1]
    cur = x_ref[0].astype(F32)
    hp = hp_ref[0].astype(F32)[HALO - 8:HALO]
    hn = hn_ref[0].astype(F32)[0:8]
    hp = jnp.where(si > 0, hp, 0.0)
    hn = jnp.where(si < ns - 1, hn, 0.0)
    ext = jnp.concatenate([hp, cur, hn], axis=0)
    n_ext = ts + 16
    w = w_ref[...]
    acc = b_ref[...] + w[2:3, :] * cur
    for k in (0, 1, 3, 4):
        shifted = pltpu.roll(ext, (2 - k) % n_ext, axis=0)[8:8 + ts]
        acc = acc + w[k:k + 1, :] * shifted
    o_ref[0] = (acc * (1.0 / (1.0 + jnp.exp(-acc)))).astype(o_ref.dtype)


def _conv(big, conv_w, conv_b, ts, tc):
    b, s, _ = big.shape
    c_total = conv_w.shape[1]
    col0 = BIG_XBC // tc
    nh = ts // HALO
    last_h = s // HALO - 1
    return pl.pallas_call(
        _conv_kernel,
        grid=(b, s // ts, c_total // tc),
        in_specs=[
            pl.BlockSpec((1, ts, tc), lambda bi, si, ci: (bi, si, col0 + ci)),
            pl.BlockSpec((1, HALO, tc),
                         lambda bi, si, ci: (bi, jnp.maximum(si * nh - 1, 0), col0 + ci)),
            pl.BlockSpec((1, HALO, tc),
                         lambda bi, si, ci: (bi, jnp.minimum((si + 1) * nh, last_h), col0 + ci)),
            pl.BlockSpec((SSM_CONV, tc), lambda bi, si, ci: (0, ci)),
            pl.BlockSpec((1, tc), lambda bi, si, ci: (0, ci)),
        ],
        out_specs=pl.BlockSpec((1, ts, tc), lambda bi, si, ci: (bi, si, ci)),
        out_shape=jax.ShapeDtypeStruct((b, s, c_total), BF16),
        compiler_params=_cparams(("parallel", "parallel", "parallel")),
        name="conv_silu",
    )(big, big, big, conv_w, conv_b)


def _ssd_kernel(x_ref, b_ref, c_ref, z_ref, pack_ref, dtr_ref, cumr_ref, sel0_ref, sel1_ref,
                dskip_ref, nw_ref, o_ref, sprev_ref, state_ref):
    phase = pl.program_id(2)
    blk = pl.program_id(3)
    nblk = pl.num_programs(3)
    tb = x_ref.shape[1]
    ncb = tb // CHUNK
    H = HEADS_PER_GROUP
    W = GROUP_W

    @pl.when(blk == 0)
    def _():
        state_ref[...] = jnp.zeros_like(state_ref)

    @pl.when(phase == 0)
    def _():
        def body(i, carry):
            r0 = pl.multiple_of(i * CHUNK, CHUNK)
            x = x_ref[0, pl.ds(r0, CHUNK), :].astype(F32)
            bm = b_ref[0, pl.ds(r0, CHUNK), :]
            e0 = jnp.dot(pack_ref[pl.ds(r0, CHUNK), :], sel0_ref[...],
                         preferred_element_type=F32)
            wa, ef = e0[:, :W], e0[:, W:]
            xa = (x * wa).astype(BF16)
            s_chunk = lax.dot_general(bm, xa, (((0,), (0,)), ((), ())),
                                      preferred_element_type=F32)
            st = state_ref[...]
            sprev_ref[blk * ncb + i] = st
            state_ref[...] = st * ef[CHUNK - 1:CHUNK, :] + s_chunk
            return carry

        lax.fori_loop(0, ncb, body, 0, unroll=SSD_UNROLL_FWD)

    @pl.when(phase == 1)
    def _():
        dskip = dskip_ref[0]
        nw = nw_ref[0]
        li = lax.broadcasted_iota(jnp.int32, (CHUNK, CHUNK), 0)
        si = lax.broadcasted_iota(jnp.int32, (CHUNK, CHUNK), 1)
        lower = li >= si
        upper = li <= si
        head_of_lane = lax.broadcasted_iota(jnp.int32, (CHUNK, W), 1) // SSM_HEAD_DIM

        def body(ii, carry):
            i = ncb - 1 - ii
            r0 = pl.multiple_of(i * CHUNK, CHUNK)
            xb16 = x_ref[0, pl.ds(r0, CHUNK), :]
            x = xb16.astype(F32)
            bm = b_ref[0, pl.ds(r0, CHUNK), :]
            cm = c_ref[0, pl.ds(r0, CHUNK), :]
            z = z_ref[0, pl.ds(r0, CHUNK), :].astype(F32)
            dtr = dtr_ref[0, :, pl.ds(r0, CHUNK)]
            cumr = cumr_ref[0, :, pl.ds(r0, CHUNK)]
            e1 = jnp.dot(pack_ref[pl.ds(r0, CHUNK), :], sel1_ref[...],
                         preferred_element_type=F32)
            ef, eb, wb = e1[:, :W], e1[:, W:2 * W], e1[:, 2 * W:3 * W]
            ccol = e1[:, 3 * W:]

            cb = lax.dot_general(cm, bm, (((1,), (1,)), ((), ())), preferred_element_type=F32)
            ms, xs = [], []
            for j in range(H):
                cf = ccol[:, j * CHUNK:(j + 1) * CHUNK]
                cbw = ccol[:, (H + j) * CHUNK:(H + j + 1) * CHUNK]
                df = jnp.exp(jnp.where(lower, cf - cumr[j:j + 1, :], NEG_BIG))
                db = jnp.exp(jnp.where(upper, cbw - cumr[H + j:H + j + 1, :], NEG_BIG))
                ms.append((cb * (df * dtr[j:j + 1, :] + db * dtr[H + j:H + j + 1, :])).astype(BF16))
                xs.append(jnp.where(head_of_lane == j, xb16, jnp.zeros_like(xb16)))
            y = jnp.dot(jnp.concatenate(ms, axis=1), jnp.concatenate(xs, axis=0),
                        preferred_element_type=F32)

            chunk_id = (nblk - 1 - blk) * ncb + i
            sp_f = sprev_ref[chunk_id].astype(BF16)
            st_b = state_ref[...]
            y = y + jnp.dot(cm, sp_f, preferred_element_type=F32) * ef
            y = y + jnp.dot(cm, st_b.astype(BF16), preferred_element_type=F32) * eb
            y = y + dskip * x

            xw = (x * wb).astype(BF16)
            s_chunk = lax.dot_general(bm, xw, (((0,), (0,)), ((), ())),
                                      preferred_element_type=F32)
            state_ref[...] = st_b * eb[0:1, :] + s_chunk

            yg = y * (z * (1.0 / (1.0 + jnp.exp(-z))))
            msq = jnp.mean(yg * yg, axis=-1, keepdims=True)
            o_ref[0, pl.ds(r0, CHUNK), :] = (yg * lax.rsqrt(msq + EPS) * nw).astype(o_ref.dtype)
            return carry

        lax.fori_loop(0, ncb, body, 0, unroll=SSD_UNROLL_BWD)


def _ssd(conv, big, pack, dtr, cumr, sel0, sel1, dskip, nw, tb):
    b, s, _ = conv.shape
    nblk = s // tb
    w = 2 * HEADS_PER_GROUP
    const = lambda bi, g, p, k: (0, 0)

    def sidx(p, k):
        return jnp.where(p == 0, k, nblk - 1 - k)

    def sidx_late(p, k):
        return jnp.where(p == 0, nblk - 1, nblk - 1 - k)

    b_col0 = SSM_D_INNER // SSM_D_STATE
    c_col0 = b_col0 + SSM_N_GROUPS
    return pl.pallas_call(
        _ssd_kernel,
        grid=(b, SSM_N_GROUPS, 2, nblk),
        in_specs=[
            pl.BlockSpec((1, tb, GROUP_W), lambda bi, g, p, k: (bi, sidx(p, k), g)),
            pl.BlockSpec((1, tb, SSM_D_STATE), lambda bi, g, p, k: (bi, sidx(p, k), b_col0 + g)),
            pl.BlockSpec((1, tb, SSM_D_STATE),
                         lambda bi, g, p, k: (bi, sidx_late(p, k), c_col0 + g)),
            pl.BlockSpec((1, tb, GROUP_W), lambda bi, g, p, k: (bi, sidx_late(p, k), g)),
            pl.BlockSpec((tb, LANES), lambda bi, g, p, k: (bi * nblk + sidx(p, k), g)),
            pl.BlockSpec((1, w, tb), lambda bi, g, p, k: (g, 0, bi * nblk + sidx_late(p, k))),
            pl.BlockSpec((1, w, tb), lambda bi, g, p, k: (g, 0, bi * nblk + sidx_late(p, k))),
            pl.BlockSpec(sel0.shape, const),
            pl.BlockSpec(sel1.shape, const),
            pl.BlockSpec((1, 1, GROUP_W), lambda bi, g, p, k: (g, 0, 0)),
            pl.BlockSpec((1, 1, GROUP_W), lambda bi, g, p, k: (g, 0, 0)),
        ],
        out_specs=pl.BlockSpec((1, tb, GROUP_W), lambda bi, g, p, k: (bi, sidx_late(p, k), g)),
        out_shape=jax.ShapeDtypeStruct((b, s, SSM_D_INNER), BF16),
        scratch_shapes=[
            pltpu.VMEM((s // CHUNK, SSM_D_STATE, GROUP_W), F32),
            pltpu.VMEM((SSM_D_STATE, GROUP_W), F32),
        ],
        compiler_params=_cparams(("parallel", "parallel", "arbitrary", "arbitrary")),
        name="ssd",
    )(conv, conv, conv, big, pack, dtr, cumr, sel0, sel1, dskip, nw)


def _rms(x, g):
    ms = jnp.mean(x * x, axis=-1, keepdims=True)
    return x * lax.rsqrt(ms + EPS) * g


def _mla_prep_kernel(ql_ref, kvl_ref, kra_ref, krb_ref, pos_ref, invf_ref, sign_ref,
                     gq_ref, gkv_ref, wq_ref, wqs_ref, wk_ref, wvt_ref, ones_ref,
                     eq_ref, ek_ref, q_ref, k_ref, vt_ref, nmax_ref):
    ang = pos_ref[...] * invf_ref[...]
    cos = jnp.cos(ang)
    sin = jnp.sin(ang) * sign_ref[...]
    scale = MLA_QK ** -0.5 * LOG2E
    cq = cos * scale
    sq = sin * scale
    qn = _rms(ql_ref[...], gq_ref[...]).astype(BF16)
    qa = jnp.dot(qn, wq_ref[...], preferred_element_type=F32)
    qb = jnp.dot(qn, wqs_ref[...], preferred_element_type=F32)
    kn = _rms(kvl_ref[...], gkv_ref[...]).astype(BF16)
    kk = jnp.dot(kn, wk_ref[...], preferred_element_type=F32)
    vvt = lax.dot_general(wvt_ref[...], kn, (((1,), (1,)), ((), ())),
                          preferred_element_type=F32) + ones_ref[...]
    kpe = kra_ref[...] * cos + krb_ref[...] * sin
    qsq, ksq = [], []
    for h in range(MLA_N_HEADS):
        sl = slice(h * HEAD_PAD, (h + 1) * HEAD_PAD)
        qh = (qa[:, sl] * cq + qb[:, sl] * sq).astype(BF16)
        kh = (kk[:, sl] + kpe).astype(BF16)
        q_ref[0, h] = qh
        k_ref[0, h] = kh
        qsq.append(qh * qh)
        ksq.append(kh * kh)
        vt_ref[0, h] = vvt[h * VT_ROWS:(h + 1) * VT_ROWS, :].astype(vt_ref.dtype)
    nsq = (jnp.dot(jnp.concatenate(qsq, axis=1), eq_ref[...], preferred_element_type=F32)
           + jnp.dot(jnp.concatenate(ksq, axis=1), ek_ref[...], preferred_element_type=F32))
    nmax_ref[0] = jnp.max(nsq.reshape(nsq.shape[0] // 8, 8, LANES), axis=0)


def _mla_prep(small, pos, invf, sign, gq, gkv, wq, wqs, wk, wvt, ones, eq, ek, b, s, tm):
    t = small.shape[0]
    nsb = s // tm
    hw = MLA_N_HEADS * HEAD_PAD
    out = jax.ShapeDtypeStruct((b, MLA_N_HEADS, s, HEAD_PAD), BF16)
    out_vt = jax.ShapeDtypeStruct((b, MLA_N_HEADS, VT_ROWS, s), BF16)
    const = lambda i: (0, 0)
    ospec = pl.BlockSpec((1, MLA_N_HEADS, tm, HEAD_PAD), lambda i: (i // nsb, 0, i % nsb, 0))
    ospec_vt = pl.BlockSpec((1, MLA_N_HEADS, VT_ROWS, tm), lambda i: (i // nsb, 0, 0, i % nsb))
    return pl.pallas_call(
        _mla_prep_kernel,
        grid=(t // tm,),
        in_specs=[
            pl.BlockSpec((tm, MLA_Q_RANK), lambda i: (i, SM_Q // MLA_Q_RANK)),
            pl.BlockSpec((tm, MLA_KV_RANK), lambda i: (i, SM_KV // MLA_KV_RANK)),
            pl.BlockSpec((tm, LANES), lambda i: (i, SM_KRA // LANES)),
            pl.BlockSpec((tm, LANES), lambda i: (i, SM_KRB // LANES)),
            pl.BlockSpec((tm, 1), lambda i: (i, 0)),
            pl.BlockSpec((1, LANES), const),
            pl.BlockSpec((1, LANES), const),
            pl.BlockSpec((1, MLA_Q_RANK), const),
            pl.BlockSpec((1, MLA_KV_RANK), const),
            pl.BlockSpec((MLA_Q_RANK, hw), const),
            pl.BlockSpec((MLA_Q_RANK, hw), const),
            pl.BlockSpec((MLA_KV_RANK, hw), const),
            pl.BlockSpec((MLA_N_HEADS * VT_ROWS, MLA_KV_RANK), const),
            pl.BlockSpec((MLA_N_HEADS * VT_ROWS, 1), const),
            pl.BlockSpec((hw, LANES), const),
            pl.BlockSpec((hw, LANES), const),
        ],
        out_specs=[ospec, ospec, ospec_vt, pl.BlockSpec((1, 8, LANES), lambda i: (i, 0, 0))],
        out_shape=[out, out, out_vt, jax.ShapeDtypeStruct((t // tm, 8, LANES), F32)],
        compiler_params=_cparams(("parallel",)),
        name="mla_prep",
    )(small, small, small, small, pos, invf, sign, gq, gkv, wq, wqs, wk, wvt, ones, eq, ek)


def _attn_kernel(nmax_ref, q_ref, k_ref, vt_ref, o_ref, s_ref, *, tq, tk, unroll):
    s = k_ref.shape[2]
    n_kv = s // tk
    n_q = s // tq
    n_tiles = n_q * n_kv
    assert n_kv % unroll == 0 and unroll % 2 == 0

    def finish(accs, qi):
        outs = [a[:MLA_V] / a[VT_ONES_ROW:VT_ONES_ROW + 1] for a in accs]
        q0 = pl.multiple_of(qi * tq, tq)
        o_ref[0, pl.ds(q0, tq), :] = jnp.concatenate(outs, axis=0).T.astype(o_ref.dtype)

    hp = pl.program_id(1)
    nmax = jnp.max(nmax_ref[...], axis=(0, 1), keepdims=True)[0]
    lane = lax.broadcasted_iota(jnp.int32, (1, LANES), 1)
    bound_sq = []
    for hh in range(2):
        qn2 = jnp.max(jnp.where(lane == 2 * hp + hh, nmax, 0.0))
        kn2 = jnp.max(jnp.where(lane == MLA_N_HEADS + 2 * hp + hh, nmax, 0.0))
        bound_sq.append(qn2 * kn2)
    small_scores = jnp.maximum(bound_sq[0], bound_sq[1]) <= SCORE_BOUND ** 2

    @pl.when(small_scores)
    def _():
        _attn_pipeline(q_ref, k_ref, vt_ref, s_ref, finish, tq=tq, tk=tk, unroll=unroll,
                       use_max=False)

    @pl.when(jnp.logical_not(small_scores))
    def _():
        _attn_pipeline(q_ref, k_ref, vt_ref, s_ref, finish, tq=tq, tk=tk, unroll=unroll,
                       use_max=True)


def _attn_pipeline(q_ref, k_ref, vt_ref, s_ref, finish, *, tq, tk, unroll, use_max):
    s = k_ref.shape[2]
    n_kv = s // tk
    n_tiles = (s // tq) * n_kv

    def scores(t, slot):
        t = jnp.minimum(t, n_tiles - 1)
        q0 = pl.multiple_of((t // n_kv) * tq, tq)
        r0 = pl.multiple_of((t % n_kv) * tk, tk)
        for hh in range(2):
            s_ref[slot, hh] = lax.dot_general(k_ref[0, hh, pl.ds(r0, tk), :],
                                              q_ref[0, hh, pl.ds(q0, tq), :],
                                              (((1,), (1,)), ((), ())),
                                              preferred_element_type=F32)

    def consume(ki, slot, stats):
        r0 = pl.multiple_of(ki * tk, tk)
        new = []
        for hh in range(2):
            m, acc = stats[2 * hh], stats[2 * hh + 1]
            vt = vt_ref[0, hh, :, pl.ds(r0, tk)]
            if use_max:
                m_new = jnp.maximum(m, jnp.max(s_ref[slot, hh], axis=0, keepdims=True))
                p = jnp.exp2(s_ref[slot, hh] - m_new).astype(BF16)
                acc = acc * jnp.exp2(m - m_new)
            else:
                m_new = m
                p = jnp.exp2(s_ref[slot, hh]).astype(BF16)
            acc = acc + jnp.dot(vt, p, preferred_element_type=F32)
            new += [m_new, acc]
        return tuple(new)

    m0 = jnp.full((1, tq), NEG_BIG, F32)
    acc0 = jnp.zeros((VT_ROWS, tq), F32)
    init = (m0, acc0, m0, acc0)

    def trip(j, stats):
        t0 = j * unroll
        qi = t0 // n_kv
        k0 = t0 % n_kv
        stats = tuple(jnp.where(k0 == 0, a, b) for a, b in zip(init, stats))
        for u in range(unroll):
            scores(t0 + u + 1, (u + 1) % 2)
            stats = consume(k0 + u, u % 2, stats)

        @pl.when(k0 + unroll == n_kv)
        def _():
            finish((stats[1], stats[3]), qi)

        return stats

    scores(0, 0)
    lax.fori_loop(0, n_tiles // unroll, trip, init)


def _attention(nmax, q, k, vt, tq, tk, unroll):
    b, h, s, _ = q.shape
    nt = nmax.shape[0] // b
    return pl.pallas_call(
        functools.partial(_attn_kernel, tq=tq, tk=tk, unroll=unroll),
        grid=(b, h // 2),
        in_specs=[
            pl.BlockSpec((nt, 8, LANES), lambda bi, hp: (bi, 0, 0)),
            pl.BlockSpec((1, 2, s, HEAD_PAD), lambda bi, hp: (bi, hp, 0, 0)),
            pl.BlockSpec((1, 2, s, HEAD_PAD), lambda bi, hp: (bi, hp, 0, 0)),
            pl.BlockSpec((1, 2, VT_ROWS, s), lambda bi, hp: (bi, hp, 0, 0)),
        ],
        out_specs=pl.BlockSpec((1, s, 2 * MLA_V), lambda bi, hp: (bi, 0, hp)),
        out_shape=jax.ShapeDtypeStruct((b, s, h * MLA_V), BF16),
        scratch_shapes=[pltpu.VMEM((2, 2, tk, tq), F32)],
        compiler_params=_cparams(("parallel", "parallel")),
        name="attention",
    )(nmax, q, k, vt)


def _merge_kernel(x_ref, y_ref, a_ref, gs_ref, gm_ref, gb_ref, wso_ref, wmo_ref, wo_ref,
                  nw_ref, o_ref):
    y_ssm = jnp.dot(y_ref[...], wso_ref[...], preferred_element_type=F32)
    y_mla = jnp.dot(a_ref[...], wmo_ref[...], preferred_element_type=F32)
    gb = gb_ref[...]
    g_ssm = 1.0 / (1.0 + jnp.exp(-(gs_ref[...].astype(F32) + gb[:, :D_MODEL])))
    g_mla = 1.0 / (1.0 + jnp.exp(-(gm_ref[...].astype(F32) + gb[:, D_MODEL:])))
    mix = (g_ssm * y_ssm + g_mla * y_mla).astype(BF16)
    mixed = jnp.dot(mix, wo_ref[...], preferred_element_type=F32)
    o_ref[...] = x_ref[...] + _rms(mixed, nw_ref[...])


def _merge(x, y, attn, big, gate_b, wso, wmo, wo, nw, tm):
    t, d = x.shape
    const = lambda i: (0, 0)
    g0 = BIG_GATE // D_MODEL
    return pl.pallas_call(
        _merge_kernel,
        grid=(t // tm,),
        in_specs=[
            pl.BlockSpec((tm, d), lambda i: (i, 0)),
            pl.BlockSpec((tm, SSM_D_INNER), lambda i: (i, 0)),
            pl.BlockSpec((tm, d), lambda i: (i, 0)),
            pl.BlockSpec((tm, d), lambda i: (i, g0)),
            pl.BlockSpec((tm, d), lambda i: (i, g0 + 1)),
            pl.BlockSpec((1, 2 * d), const),
            pl.BlockSpec((SSM_D_INNER, d), const),
            pl.BlockSpec((d, d), const),
            pl.BlockSpec((d, d), const),
            pl.BlockSpec((1, d), const),
        ],
        out_specs=pl.BlockSpec((tm, d), lambda i: (i, 0)),
        out_shape=jax.ShapeDtypeStruct((t, d), F32),
        compiler_params=_cparams(("parallel",)),
        name="merge",
    )(x, y, attn, big, big, gate_b, wso, wmo, wo, nw)


def _mlp_kernel(x_ref, g1_ref, wu_ref, wd_ref, g2_ref, o_ref, h_ref, acc_ref):
    j = pl.program_id(1)

    @pl.when(j == 0)
    def _():
        h_ref[...] = _rms(x_ref[...], g1_ref[...]).astype(h_ref.dtype)
        acc_ref[...] = jnp.zeros_like(acc_ref)

    u = jnp.dot(h_ref[...], wu_ref[...], preferred_element_type=F32)
    r = jnp.maximum(u, 0.0)
    acc_ref[...] += jnp.dot((r * r).astype(BF16), wd_ref[...], preferred_element_type=F32)

    @pl.when(j == pl.num_programs(1) - 1)
    def _():
        o_ref[...] = x_ref[...] + _rms(acc_ref[...], g2_ref[...])


def _mlp(x, g1, wu, wd, g2, tm, tf):
    t, d = x.shape
    f = wu.shape[1]
    return pl.pallas_call(
        _mlp_kernel,
        grid=(t // tm, f // tf),
        in_specs=[
            pl.BlockSpec((tm, d), lambda i, j: (i, 0)),
            pl.BlockSpec((1, d), lambda i, j: (0, 0)),
            pl.BlockSpec((d, tf), lambda i, j: (0, j)),
            pl.BlockSpec((tf, d), lambda i, j: (j, 0)),
            pl.BlockSpec((1, d), lambda i, j: (0, 0)),
        ],
        out_specs=pl.BlockSpec((tm, d), lambda i, j: (i, 0)),
        out_shape=jax.ShapeDtypeStruct((t, d), F32),
        scratch_shapes=[pltpu.VMEM((tm, d), BF16), pltpu.VMEM((tm, d), F32)],
        compiler_params=_cparams(("parallel", "arbitrary")),
        name="mlp",
    )(x, g1, wu, wd, g2)


def _group_major(fwd, bwd):
    lead = fwd.shape[:-1]
    f = fwd.reshape(lead + (SSM_N_GROUPS, HEADS_PER_GROUP))
    bw = bwd.reshape(lead + (SSM_N_GROUPS, HEADS_PER_GROUP))
    return jnp.concatenate([f, bw], axis=-1).reshape(lead + (2 * SSM_N_HEADS,))


def _pad_cols(w, n):
    return jnp.pad(w, ((0, 0), (0, n - w.shape[1])))


def _pick(n, prefs):
    for p in prefs:
        if n % p == 0:
            return p
    raise ValueError(f"no tile in {prefs} divides {n}")


def kernel(x, positions, norm_mix_pre, w_in, conv_w, conv_b, dt_bias_fwd, dt_bias_bwd, a_log_fwd, a_log_bwd, d_skip, ssm_norm_w, w_ssm_out, q_a_norm, w_q_b, kv_a_norm, w_kv_b, w_mla_out, gate_b, w_out, norm_mix_post, norm_mlp_pre, w_up, w_down, norm_mlp_post):
    b, s, d = x.shape
    t = b * s
    depth = w_in.shape[0]
    assert d == D_MODEL and s % 512 == 0
    half = MLA_ROPE // 2

    inv_freq = ROPE_BASE ** (-np.arange(0, MLA_ROPE, 2, dtype=np.float32) / MLA_ROPE)
    invf = np.zeros((1, LANES), np.float32)
    invf[0, MLA_NOPE:MLA_NOPE + half] = inv_freq
    invf[0, MLA_NOPE + half:MLA_NOPE + MLA_ROPE] = inv_freq
    sign = np.zeros((1, LANES), np.float32)
    sign[0, MLA_NOPE:MLA_NOPE + half] = -1.0
    sign[0, MLA_NOPE + half:MLA_NOPE + MLA_ROPE] = 1.0
    ones = np.zeros((MLA_N_HEADS, VT_ROWS, 1), np.float32)
    ones[:, VT_ONES_ROW] = 1.0
    ones = ones.reshape(MLA_N_HEADS * VT_ROWS, 1)
    head_of_col = np.arange(MLA_N_HEADS * HEAD_PAD) // HEAD_PAD
    eq = (head_of_col[:, None] == np.arange(LANES)[None, :]).astype(np.float32)
    ek = (head_of_col[:, None] + MLA_N_HEADS == np.arange(LANES)[None, :]).astype(np.float32)
    eq, ek = jnp.asarray(eq, BF16), jnp.asarray(ek, BF16)
    invf, sign, ones = jnp.asarray(invf), jnp.asarray(sign), jnp.asarray(ones)
    perm, sel0, sel1 = (jnp.asarray(m, BF16) for m in _pack_constants())
    pos = positions.astype(F32).reshape(t, 1)

    xt = x.reshape(t, d)
    for l in range(depth):
        wz, wxbc, wdt, wql, wkvl, wkr, wg = jnp.split(
            w_in[l], np.cumsum([SSM_D_INNER, SSM_CONV_DIM, 2 * SSM_N_HEADS, MLA_Q_RANK,
                                MLA_KV_RANK, MLA_ROPE])[:].tolist(), axis=1)
        w_big = jnp.concatenate([wz, wxbc, wg], axis=1).astype(BF16)
        wdt_gm = _group_major(wdt[:, :SSM_N_HEADS], wdt[:, SSM_N_HEADS:])
        wkr_sw = jnp.concatenate([wkr[:, half:], wkr[:, :half]], axis=1)
        zpad = jnp.zeros((d, MLA_NOPE), F32)
        w_small = jnp.concatenate([
            wql, wkvl,
            _pad_cols(jnp.concatenate([zpad, wkr], axis=1), LANES),
            _pad_cols(jnp.concatenate([zpad, wkr_sw], axis=1), LANES),
            _pad_cols(wdt_gm, LANES)], axis=1).astype(BF16)
        dt_bias = _pad_cols(_group_major(dt_bias_fwd[l], dt_bias_bwd[l])[None, :], LANES)
        a_log = _pad_cols(_group_major(a_log_fwd[l], a_log_bwd[l])[None, :], LANES)
        dskip = jnp.repeat(d_skip[l], SSM_HEAD_DIM).reshape(SSM_N_GROUPS, 1, GROUP_W)
        nw_ssm = ssm_norm_w[l].reshape(SSM_N_GROUPS, 1, GROUP_W)

        wq3 = w_q_b[l].reshape(MLA_Q_RANK, MLA_N_HEADS, MLA_QK)
        q_nope, q_pe = wq3[..., :MLA_NOPE], wq3[..., MLA_NOPE:]
        q_pe_sw = jnp.concatenate([q_pe[..., half:], q_pe[..., :half]], axis=-1)
        hz = jnp.zeros((MLA_Q_RANK, MLA_N_HEADS, HEAD_PAD - MLA_QK), F32)
        wq = jnp.concatenate([q_nope, q_pe, hz], axis=-1).reshape(MLA_Q_RANK, -1).astype(BF16)
        wqs = jnp.concatenate([jnp.zeros_like(q_nope), q_pe_sw, hz],
                              axis=-1).reshape(MLA_Q_RANK, -1).astype(BF16)
        wkv3 = w_kv_b[l].reshape(MLA_KV_RANK, MLA_N_HEADS, MLA_NOPE + MLA_V)
        k_nope, v_w = wkv3[..., :MLA_NOPE], wkv3[..., MLA_NOPE:]
        z64 = jnp.zeros_like(k_nope)
        wk = jnp.concatenate([k_nope, z64], axis=-1).reshape(MLA_KV_RANK, -1).astype(BF16)
        wvt = jnp.pad(jnp.transpose(v_w, (1, 2, 0)), ((0, 0), (0, VT_ROWS - MLA_V), (0, 0)))
        wvt = wvt.reshape(MLA_N_HEADS * VT_ROWS, MLA_KV_RANK).astype(BF16)

        tm = _pick(t, (1024, 512))
        gain = norm_mix_pre[l][None, :]
        big = _norm_matmul(xt, gain, w_big, BF16, tm, 1024)
        small = _norm_matmul(xt, gain, w_small, F32, tm, SM_W)

        pack, dtr, cumr = _dt_prep(small, dt_bias, a_log, perm, _pick(t, (1024, 512)))
        big3 = big.reshape(b, s, BIG_W)
        conv = _conv(big3, conv_w[l], conv_b[l][None, :], 512, 512)
        y = _ssd(conv, big3, pack, dtr, cumr, sel0, sel1, dskip, nw_ssm, _pick(s, (1024, 512)))

        q, k, vt, nmax = _mla_prep(small, pos, invf, sign, q_a_norm[l][None, :],
                                   kv_a_norm[l][None, :], wq, wqs, wk, wvt, ones, eq, ek, b, s, 512)
        attn = _attention(nmax, q, k, vt, ATT_TQ, ATT_TK, ATT_UNROLL)

        x1 = _merge(xt, y.reshape(t, SSM_D_INNER), attn.reshape(t, d), big,
                    gate_b[l][None, :], w_ssm_out[l].astype(BF16), w_mla_out[l].astype(BF16),
                    w_out[l].astype(BF16), norm_mix_post[l][None, :], 512)

        xt = _mlp(x1, norm_mlp_pre[l][None, :], w_up[l].astype(BF16), w_down[l].astype(BF16),
                  norm_mlp_post[l][None, :], tm, 1024)
    return xt.reshape(b, s, d)
```

```python
import functools
import math

import numpy as np
import jax
import jax.numpy as jnp
from jax import lax
from jax.experimental import pallas as pl
from jax.experimental.pallas import tpu as pltpu

F32 = jnp.float32
BF16 = jnp.bfloat16

D_MODEL = 1024
SSM_D_INNER = 2048
SSM_HEAD_DIM = 64
SSM_N_HEADS = 32
SSM_N_GROUPS = 8
SSM_D_STATE = 128
SSM_CONV = 5
CHUNK = 128
SSM_CONV_DIM = SSM_D_INNER + 2 * SSM_N_GROUPS * SSM_D_STATE
GROUP_W = SSM_D_INNER // SSM_N_GROUPS
HEADS_PER_GROUP = SSM_N_HEADS // SSM_N_GROUPS
MLA_N_HEADS = 16
MLA_Q_RANK = 256
MLA_KV_RANK = 256
MLA_NOPE = 64
MLA_ROPE = 32
MLA_V = 64
MLA_QK = MLA_NOPE + MLA_ROPE
ROPE_BASE = 10000.0
D_FF = 4 * D_MODEL
EPS = 1e-6
LANES = 128
HEAD_PAD = 128
VT_ROWS = 80
VT_ONES_ROW = MLA_V
LOG2E = 1.4426950408889634
NEG_BIG = -1e30
SCORE_BOUND = 80.0

BIG_Z = 0
BIG_XBC = SSM_D_INNER
BIG_GATE = SSM_D_INNER + SSM_CONV_DIM
BIG_W = BIG_GATE + 2 * D_MODEL
SM_Q = 0
SM_KV = 256
SM_KRA = 512
SM_KRB = 640
SM_DT = 768
SM_W = 896

VMEM_LIMIT = 52 * 1024 * 1024

ATT_TQ = 512
ATT_TK = 512
ATT_UNROLL = 8
SSD_UNROLL_FWD = 8
SSD_UNROLL_BWD = 8


def _cparams(sem):
    return pltpu.CompilerParams(dimension_semantics=sem, vmem_limit_bytes=VMEM_LIMIT)


def _norm_matmul_kernel(x_ref, g_ref, w_ref, o_ref, h_ref):
    @pl.when(pl.program_id(1) == 0)
    def _():
        x = x_ref[...]
        ms = jnp.mean(x * x, axis=-1, keepdims=True)
        h_ref[...] = (x * lax.rsqrt(ms + EPS) * g_ref[...]).astype(h_ref.dtype)

    o_ref[...] = jnp.dot(h_ref[...], w_ref[...],
                         preferred_element_type=F32).astype(o_ref.dtype)


def _norm_matmul(x, gain, w, out_dtype, tm, tn):
    t, d = x.shape
    n = w.shape[1]
    return pl.pallas_call(
        _norm_matmul_kernel,
        grid=(t // tm, n // tn),
        in_specs=[
            pl.BlockSpec((tm, d), lambda i, j: (i, 0)),
            pl.BlockSpec((1, d), lambda i, j: (0, 0)),
            pl.BlockSpec((d, tn), lambda i, j: (0, j)),
        ],
        out_specs=pl.BlockSpec((tm, tn), lambda i, j: (i, j)),
        out_shape=jax.ShapeDtypeStruct((t, n), out_dtype),
        scratch_shapes=[pltpu.VMEM((tm, d), BF16)],
        compiler_params=_cparams(("parallel", "arbitrary")),
        name="in_proj",
    )(x, gain, w)


def _split3(x):
    hi = x.astype(BF16)
    r = x - hi.astype(F32)
    mid = r.astype(BF16)
    lo = (r - mid.astype(F32)).astype(BF16)
    return [hi, mid, lo]


def _dt_prep_kernel(raw_ref, bias_ref, alog_ref, perm_ref, pack_ref, dtr_ref, cumr_ref):
    tb = raw_ref.shape[0]
    v = raw_ref[...] + bias_ref[...]
    dt = jnp.maximum(v, 0.0) + jnp.log(1.0 + jnp.exp(-jnp.abs(v)))
    la = dt * (-jnp.exp(alog_ref[...]))
    lane = lax.broadcasted_iota(jnp.int32, (CHUNK, LANES), 1)
    is_fwd = (lane % (2 * HEADS_PER_GROUP)) < HEADS_PER_GROUP
    row = lax.broadcasted_iota(jnp.int32, (CHUNK, CHUNK), 0)
    col = lax.broadcasted_iota(jnp.int32, (CHUNK, CHUNK), 1)
    lower = (col <= row).astype(F32)
    upper = (col >= row).astype(F32)
    cums, wgts = [], []
    for c in range(tb // CHUNK):
        la_c = la[c * CHUNK:(c + 1) * CHUNK]
        la_f = jnp.where(is_fwd, la_c, 0.0)
        la_b = jnp.where(is_fwd, 0.0, la_c)
        cum_c = (jnp.dot(lower, la_f, precision=lax.Precision.HIGHEST,
                         preferred_element_type=F32)
                 + jnp.dot(upper, la_b, precision=lax.Precision.HIGHEST,
                           preferred_element_type=F32))
        edge = jnp.where(is_fwd[:1], cum_c[CHUNK - 1:CHUNK], cum_c[0:1])
        cums.append(cum_c)
        wgts.append(dt[c * CHUNK:(c + 1) * CHUNK] * jnp.exp(edge - cum_c))
    cum = jnp.concatenate(cums, axis=0)
    wgt = jnp.concatenate(wgts, axis=0)
    parts = _split3(jnp.exp(cum)) + _split3(wgt) + _split3(cum)
    pack_ref[...] = jnp.dot(jnp.concatenate(parts, axis=1), perm_ref[...],
                            preferred_element_type=F32).astype(pack_ref.dtype)
    dt_t = dt.T
    cum_t = cum.T
    w = 2 * HEADS_PER_GROUP
    for g in range(SSM_N_GROUPS):
        dtr_ref[g] = dt_t[g * w:(g + 1) * w, :]
        cumr_ref[g] = cum_t[g * w:(g + 1) * w, :]


def _dt_prep(small, bias, alog, perm, tb):
    t = small.shape[0]
    w = 2 * HEADS_PER_GROUP
    row_shape = jax.ShapeDtypeStruct((SSM_N_GROUPS, w, t), F32)
    return pl.pallas_call(
        _dt_prep_kernel,
        grid=(t // tb,),
        in_specs=[
            pl.BlockSpec((tb, LANES), lambda i: (i, SM_DT // LANES)),
            pl.BlockSpec((1, LANES), lambda i: (0, 0)),
            pl.BlockSpec((1, LANES), lambda i: (0, 0)),
            pl.BlockSpec(perm.shape, lambda i: (0, 0)),
        ],
        out_specs=[
            pl.BlockSpec((tb, SSM_N_GROUPS * LANES), lambda i: (i, 0)),
            pl.BlockSpec((SSM_N_GROUPS, w, tb), lambda i: (0, 0, i)),
            pl.BlockSpec((SSM_N_GROUPS, w, tb), lambda i: (0, 0, i)),
        ],
        out_shape=[jax.ShapeDtypeStruct((t, SSM_N_GROUPS * LANES), BF16), row_shape, row_shape],
        compiler_params=_cparams(("parallel",)),
        name="dt_prep",
    )(small, bias, alog, perm)


def _pack_constants():
    w = 2 * HEADS_PER_GROUP
    n_parts = 9
    perm = np.zeros((n_parts * LANES, SSM_N_GROUPS * LANES), np.float32)
    for q in range(n_parts):
        for g in range(SSM_N_GROUPS):
            for i in range(w):
                perm[q * LANES + g * w + i, g * LANES + q * w + i] = 1.0

    def expand_cols(first_part, j0):
        m = np.zeros((LANES, GROUP_W), np.float32)
        for p in range(GROUP_W):
            for q in range(3):
                m[(first_part + q) * w + j0 + p // SSM_HEAD_DIM, p] = 1.0
        return m

    def bcast_cols(first_part):
        m = np.zeros((LANES, w * CHUNK), np.float32)
        for i in range(w):
            for q in range(3):
                m[(first_part + q) * w + i, i * CHUNK:(i + 1) * CHUNK] = 1.0
        return m

    h = HEADS_PER_GROUP
    sel0 = np.concatenate([expand_cols(3, 0), expand_cols(0, 0)], axis=1)
    sel1 = np.concatenate([expand_cols(0, 0), expand_cols(0, h), expand_cols(3, h),
                           bcast_cols(6)], axis=1)
    return perm, sel0, sel1


HALO = 16


CONV_WIN = CHUNK + 2 * HALO


CONV_SIDE_TAPS = tuple(k for k in range(SSM_CONV) if k != SSM_CONV // 2)


def _conv_select_matrix():
    m = np.zeros((CHUNK, len(CONV_SIDE_TAPS) * CONV_WIN), np.float32)
    for n, k in enumerate(CONV_SIDE_TAPS):
        for i in range(CHUNK):
            m[i, n * CONV_WIN + HALO + i + k - SSM_CONV // 2] = 1.0
    return m


def _conv_kernel(x_ref, hp_ref, hn_ref, w_ref, b_ref, sel_ref, o_ref):
    si = pl.program_id(1)
    ns = pl.num_programs(1)
    ts = x_ref.shape[1]
    zero = jnp.zeros((HALO, x_ref.shape[2]), x_ref.dtype)
    hp = jnp.where(si > 0, hp_ref[0], zero)
    hn = jnp.where(si < ns - 1, hn_ref[0], zero)
    ext = jnp.concatenate([hp, x_ref[0], hn], axis=0)
    wf = w_ref[...]
    w = wf.astype(BF16)
    w_mid = wf[SSM_CONV // 2:SSM_CONV // 2 + 1, :]
    bias = b_ref[...]
    sel = sel_ref[...]
    for r in range(ts // CHUNK):
        win = ext[r * CHUNK:r * CHUNK + CONV_WIN]
        stacked = jnp.concatenate([win * w[k:k + 1, :] for k in CONV_SIDE_TAPS], axis=0)
        mid = x_ref[0, r * CHUNK:(r + 1) * CHUNK, :].astype(F32) * w_mid
        acc = jnp.dot(sel, stacked, preferred_element_type=F32) + (mid + bias)
        half = 0.5 * acc
        o_ref[0, r * CHUNK:(r + 1) * CHUNK, :] = (half + half * jnp.tanh(half)).astype(o_ref.dtype)


def _conv(big, conv_w, conv_b, ts, tc):
    b, s, _ = big.shape
    sel = jnp.asarray(_conv_select_matrix(), BF16)
    c_total = conv_w.shape[1]
    col0 = BIG_XBC // tc
    nh = ts // HALO
    last_h = s // HALO - 1
    return pl.pallas_call(
        _conv_kernel,
        grid=(b, s // ts, c_total // tc),
        in_specs=[
            pl.BlockSpec((1, ts, tc), lambda bi, si, ci: (bi, si, col0 + ci)),
            pl.BlockSpec((1, HALO, tc),
                         lambda bi, si, ci: (bi, jnp.maximum(si * nh - 1, 0), col0 + ci)),
            pl.BlockSpec((1, HALO, tc),
                         lambda bi, si, ci: (bi, jnp.minimum((si + 1) * nh, last_h), col0 + ci)),
            pl.BlockSpec((SSM_CONV, tc), lambda bi, si, ci: (0, ci)),
            pl.BlockSpec((1, tc), lambda bi, si, ci: (0, ci)),
            pl.BlockSpec(sel.shape, lambda bi, si, ci: (0, 0)),
        ],
        out_specs=pl.BlockSpec((1, ts, tc), lambda bi, si, ci: (bi, si, ci)),
        out_shape=jax.ShapeDtypeStruct((b, s, c_total), BF16),
        compiler_params=_cparams(("parallel", "parallel", "parallel")),
        name="conv_silu",
    )(big, big, big, conv_w, conv_b, sel)


def _ssd_kernel(x_ref, b_ref, c_ref, z_ref, pack_ref, dtr_ref, cumr_ref, sel0_ref, sel1_ref,
                dskip_ref, nw_ref, o_ref, sprev_ref, state_ref):
    phase = pl.program_id(2)
    blk = pl.program_id(3)
    nblk = pl.num_programs(3)
    tb = x_ref.shape[1]
    ncb = tb // CHUNK
    H = HEADS_PER_GROUP
    W = GROUP_W

    @pl.when(blk == 0)
    def _():
        state_ref[...] = jnp.zeros_like(state_ref)

    @pl.when(phase == 0)
    def _():
        def body(i, carry):
            r0 = pl.multiple_of(i * CHUNK, CHUNK)
            x = x_ref[0, pl.ds(r0, CHUNK), :].astype(F32)
            bm = b_ref[0, pl.ds(r0, CHUNK), :]
            e0 = jnp.dot(pack_ref[pl.ds(r0, CHUNK), :], sel0_ref[...],
                         preferred_element_type=F32)
            wa, ef = e0[:, :W], e0[:, W:]
            xa = (x * wa).astype(BF16)
            s_chunk = lax.dot_general(bm, xa, (((0,), (0,)), ((), ())),
                                      preferred_element_type=F32)
            st = state_ref[...]
            sprev_ref[blk * ncb + i] = st
            state_ref[...] = st * ef[CHUNK - 1:CHUNK, :] + s_chunk
            return carry

        lax.fori_loop(0, ncb, body, 0, unroll=SSD_UNROLL_FWD)

    @pl.when(phase == 1)
    def _():
        dskip = dskip_ref[0]
        nw = nw_ref[0]
        li = lax.broadcasted_iota(jnp.int32, (CHUNK, CHUNK), 0)
        si = lax.broadcasted_iota(jnp.int32, (CHUNK, CHUNK), 1)
        lower = li >= si
        upper = li <= si
        head_of_lane = lax.broadcasted_iota(jnp.int32, (CHUNK, W), 1) // SSM_HEAD_DIM

        def body(ii, carry):
            i = ncb - 1 - ii
            r0 = pl.multiple_of(i * CHUNK, CHUNK)
            xb16 = x_ref[0, pl.ds(r0, CHUNK), :]
            x = xb16.astype(F32)
            bm = b_ref[0, pl.ds(r0, CHUNK), :]
            cm = c_ref[0, pl.ds(r0, CHUNK), :]
            z = z_ref[0, pl.ds(r0, CHUNK), :].astype(F32)
            dtr = dtr_ref[0, :, pl.ds(r0, CHUNK)]
            cumr = cumr_ref[0, :, pl.ds(r0, CHUNK)]
            e1 = jnp.dot(pack_ref[pl.ds(r0, CHUNK), :], sel1_ref[...],
                         preferred_element_type=F32)
            ef, eb, wb = e1[:, :W], e1[:, W:2 * W], e1[:, 2 * W:3 * W]
            ccol = e1[:, 3 * W:]

            cb = lax.dot_general(cm, bm, (((1,), (1,)), ((), ())), preferred_element_type=F32)
            ms, xs = [], []
            for j in range(H):
                cf = ccol[:, j * CHUNK:(j + 1) * CHUNK]
                cbw = ccol[:, (H + j) * CHUNK:(H + j + 1) * CHUNK]
                df = jnp.exp(jnp.where(lower, cf - cumr[j:j + 1, :], NEG_BIG))
                db = jnp.exp(jnp.where(upper, cbw - cumr[H + j:H + j + 1, :], NEG_BIG))
                ms.append((cb * (df * dtr[j:j + 1, :] + db * dtr[H + j:H + j + 1, :])).astype(BF16))
                xs.append(jnp.where(head_of_lane == j, xb16, jnp.zeros_like(xb16)))
            y = jnp.dot(jnp.concatenate(ms, axis=1), jnp.concatenate(xs, axis=0),
                        preferred_element_type=F32)

            chunk_id = (nblk - 1 - blk) * ncb + i
            sp_f = sprev_ref[chunk_id].astype(BF16)
            st_b = state_ref[...]
            y = y + jnp.dot(cm, sp_f, preferred_element_type=F32) * ef
            y = y + jnp.dot(cm, st_b.astype(BF16), preferred_element_type=F32) * eb
            y = y + dskip * x

            xw = (x * wb).astype(BF16)
            s_chunk = lax.dot_general(bm, xw, (((0,), (0,)), ((), ())),
                                      preferred_element_type=F32)
            state_ref[...] = st_b * eb[0:1, :] + s_chunk

            yg = y * (z * (1.0 / (1.0 + jnp.exp(-z))))
            msq = jnp.mean(yg * yg, axis=-1, keepdims=True)
            o_ref[0, pl.ds(r0, CHUNK), :] = (yg * lax.rsqrt(msq + EPS) * nw).astype(o_ref.dtype)
            return carry

        lax.fori_loop(0, ncb, body, 0, unroll=SSD_UNROLL_BWD)


def _ssd(conv, big, pack, dtr, cumr, sel0, sel1, dskip, nw, tb):
    b, s, _ = conv.shape
    nblk = s // tb
    w = 2 * HEADS_PER_GROUP
    const = lambda bi, g, p, k: (0, 0)

    def sidx(p, k):
        return jnp.where(p == 0, k, nblk - 1 - k)

    def sidx_late(p, k):
        return jnp.where(p == 0, nblk - 1, nblk - 1 - k)

    b_col0 = SSM_D_INNER // SSM_D_STATE
    c_col0 = b_col0 + SSM_N_GROUPS
    return pl.pallas_call(
        _ssd_kernel,
        grid=(b, SSM_N_GROUPS, 2, nblk),
        in_specs=[
            pl.BlockSpec((1, tb, GROUP_W), lambda bi, g, p, k: (bi, sidx(p, k), g)),
            pl.BlockSpec((1, tb, SSM_D_STATE), lambda bi, g, p, k: (bi, sidx(p, k), b_col0 + g)),
            pl.BlockSpec((1, tb, SSM_D_STATE),
                         lambda bi, g, p, k: (bi, sidx_late(p, k), c_col0 + g)),
            pl.BlockSpec((1, tb, GROUP_W), lambda bi, g, p, k: (bi, sidx_late(p, k), g)),
            pl.BlockSpec((tb, LANES), lambda bi, g, p, k: (bi * nblk + sidx(p, k), g)),
            pl.BlockSpec((1, w, tb), lambda bi, g, p, k: (g, 0, bi * nblk + sidx_late(p, k))),
            pl.BlockSpec((1, w, tb), lambda bi, g, p, k: (g, 0, bi * nblk + sidx_late(p, k))),
            pl.BlockSpec(sel0.shape, const),
            pl.BlockSpec(sel1.shape, const),
            pl.BlockSpec((1, 1, GROUP_W), lambda bi, g, p, k: (g, 0, 0)),
            pl.BlockSpec((1, 1, GROUP_W), lambda bi, g, p, k: (g, 0, 0)),
        ],
        out_specs=pl.BlockSpec((1, tb, GROUP_W), lambda bi, g, p, k: (bi, sidx_late(p, k), g)),
        out_shape=jax.ShapeDtypeStruct((b, s, SSM_D_INNER), BF16),
        scratch_shapes=[
            pltpu.VMEM((s // CHUNK, SSM_D_STATE, GROUP_W), F32),
            pltpu.VMEM((SSM_D_STATE, GROUP_W), F32),
        ],
        compiler_params=_cparams(("parallel", "parallel", "arbitrary", "arbitrary")),
        name="ssd",
    )(conv, conv, conv, big, pack, dtr, cumr, sel0, sel1, dskip, nw)


def _rms(x, g):
    ms = jnp.mean(x * x, axis=-1, keepdims=True)
    return x * lax.rsqrt(ms + EPS) * g


def _mla_prep_kernel(ql_ref, kvl_ref, kra_ref, krb_ref, pos_ref, invf_ref, sign_ref,
                     gq_ref, gkv_ref, wq_ref, wqs_ref, wk_ref, wvt_ref, ones_ref,
                     eq_ref, ek_ref, q_ref, k_ref, vt_ref, nmax_ref):
    ang = pos_ref[...] * invf_ref[...]
    cos = jnp.cos(ang)
    sin = jnp.sin(ang) * sign_ref[...]
    scale = MLA_QK ** -0.5 * LOG2E
    cq = cos * scale
    sq = sin * scale
    qn = _rms(ql_ref[...], gq_ref[...]).astype(BF16)
    qa = jnp.dot(qn, wq_ref[...], preferred_element_type=F32)
    qb = jnp.dot(qn, wqs_ref[...], preferred_element_type=F32)
    kn = _rms(kvl_ref[...], gkv_ref[...]).astype(BF16)
    kk = jnp.dot(kn, wk_ref[...], preferred_element_type=F32)
    vvt = lax.dot_general(wvt_ref[...], kn, (((1,), (1,)), ((), ())),
                          preferred_element_type=F32) + ones_ref[...]
    kpe = kra_ref[...] * cos + krb_ref[...] * sin
    qsq, ksq = [], []
    for h in range(MLA_N_HEADS):
        sl = slice(h * HEAD_PAD, (h + 1) * HEAD_PAD)
        qh = (qa[:, sl] * cq + qb[:, sl] * sq).astype(BF16)
        kh = (kk[:, sl] + kpe).astype(BF16)
        q_ref[0, h] = qh
        k_ref[0, h] = kh
        qsq.append(qh * qh)
        ksq.append(kh * kh)
        vt_ref[0, h] = vvt[h * VT_ROWS:(h + 1) * VT_ROWS, :].astype(vt_ref.dtype)
    nsq = (jnp.dot(jnp.concatenate(qsq, axis=1), eq_ref[...], preferred_element_type=F32)
           + jnp.dot(jnp.concatenate(ksq, axis=1), ek_ref[...], preferred_element_type=F32))
    nmax_ref[0] = jnp.max(nsq.reshape(nsq.shape[0] // 8, 8, LANES), axis=0)


def _mla_prep(small, pos, invf, sign, gq, gkv, wq, wqs, wk, wvt, ones, eq, ek, b, s, tm):
    t = small.shape[0]
    nsb = s // tm
    hw = MLA_N_HEADS * HEAD_PAD
    out = jax.ShapeDtypeStruct((b, MLA_N_HEADS, s, HEAD_PAD), BF16)
    out_vt = jax.ShapeDtypeStruct((b, MLA_N_HEADS, VT_ROWS, s), BF16)
    const = lambda i: (0, 0)
    ospec = pl.BlockSpec((1, MLA_N_HEADS, tm, HEAD_PAD), lambda i: (i // nsb, 0, i % nsb, 0))
    ospec_vt = pl.BlockSpec((1, MLA_N_HEADS, VT_ROWS, tm), lambda i: (i // nsb, 0, 0, i % nsb))
    return pl.pallas_call(
        _mla_prep_kernel,
        grid=(t // tm,),
        in_specs=[
            pl.BlockSpec((tm, MLA_Q_RANK), lambda i: (i, SM_Q // MLA_Q_RANK)),
            pl.BlockSpec((tm, MLA_KV_RANK), lambda i: (i, SM_KV // MLA_KV_RANK)),
            pl.BlockSpec((tm, LANES), lambda i: (i, SM_KRA // LANES)),
            pl.BlockSpec((tm, LANES), lambda i: (i, SM_KRB // LANES)),
            pl.BlockSpec((tm, 1), lambda i: (i, 0)),
            pl.BlockSpec((1, LANES), const),
            pl.BlockSpec((1, LANES), const),
            pl.BlockSpec((1, MLA_Q_RANK), const),
            pl.BlockSpec((1, MLA_KV_RANK), const),
            pl.BlockSpec((MLA_Q_RANK, hw), const),
            pl.BlockSpec((MLA_Q_RANK, hw), const),
            pl.BlockSpec((MLA_KV_RANK, hw), const),
            pl.BlockSpec((MLA_N_HEADS * VT_ROWS, MLA_KV_RANK), const),
            pl.BlockSpec((MLA_N_HEADS * VT_ROWS, 1), const),
            pl.BlockSpec((hw, LANES), const),
            pl.BlockSpec((hw, LANES), const),
        ],
        out_specs=[ospec, ospec, ospec_vt, pl.BlockSpec((1, 8, LANES), lambda i: (i, 0, 0))],
        out_shape=[out, out, out_vt, jax.ShapeDtypeStruct((t // tm, 8, LANES), F32)],
        compiler_params=_cparams(("parallel",)),
        name="mla_prep",
    )(small, small, small, small, pos, invf, sign, gq, gkv, wq, wqs, wk, wvt, ones, eq, ek)


def _attn_kernel(nmax_ref, q_ref, k_ref, vt_ref, o_ref, s_ref, *, tq, tk, unroll):
    s = k_ref.shape[2]
    n_kv = s // tk
    n_q = s // tq
    n_tiles = n_q * n_kv
    assert n_kv % unroll == 0 and unroll % 2 == 0

    def finish(accs, qi):
        outs = [a[:MLA_V] / a[VT_ONES_ROW:VT_ONES_ROW + 1] for a in accs]
        q0 = pl.multiple_of(qi * tq, tq)
        o_ref[0, pl.ds(q0, tq), :] = jnp.concatenate(outs, axis=0).T.astype(o_ref.dtype)

    hp = pl.program_id(1)
    nmax = jnp.max(nmax_ref[...], axis=(0, 1), keepdims=True)[0]
    lane = lax.broadcasted_iota(jnp.int32, (1, LANES), 1)
    bound_sq = []
    for hh in range(2):
        qn2 = jnp.max(jnp.where(lane == 2 * hp + hh, nmax, 0.0))
        kn2 = jnp.max(jnp.where(lane == MLA_N_HEADS + 2 * hp + hh, nmax, 0.0))
        bound_sq.append(qn2 * kn2)
    small_scores = jnp.maximum(bound_sq[0], bound_sq[1]) <= SCORE_BOUND ** 2

    @pl.when(small_scores)
    def _():
        _attn_pipeline(q_ref, k_ref, vt_ref, s_ref, finish, tq=tq, tk=tk, unroll=unroll,
                       use_max=False)

    @pl.when(jnp.logical_not(small_scores))
    def _():
        _attn_pipeline(q_ref, k_ref, vt_ref, s_ref, finish, tq=tq, tk=tk, unroll=unroll,
                       use_max=True)


def _attn_pipeline(q_ref, k_ref, vt_ref, s_ref, finish, *, tq, tk, unroll, use_max):
    s = k_ref.shape[2]
    n_kv = s // tk
    n_tiles = (s // tq) * n_kv

    def scores(t, slot):
        t = jnp.minimum(t, n_tiles - 1)
        q0 = pl.multiple_of((t // n_kv) * tq, tq)
        r0 = pl.multiple_of((t % n_kv) * tk, tk)
        for hh in range(2):
            s_ref[slot, hh] = lax.dot_general(k_ref[0, hh, pl.ds(r0, tk), :],
                                              q_ref[0, hh, pl.ds(q0, tq), :],
                                              (((1,), (1,)), ((), ())),
                                              preferred_element_type=F32)

    def consume(ki, slot, stats):
        r0 = pl.multiple_of(ki * tk, tk)
        new = []
        for hh in range(2):
            m, acc = stats[2 * hh], stats[2 * hh + 1]
            vt = vt_ref[0, hh, :, pl.ds(r0, tk)]
            if use_max:
                m_new = jnp.maximum(m, jnp.max(s_ref[slot, hh], axis=0, keepdims=True))
                p = jnp.exp2(s_ref[slot, hh] - m_new).astype(BF16)
                acc = acc * jnp.exp2(m - m_new)
            else:
                m_new = m
                p = jnp.exp2(s_ref[slot, hh]).astype(BF16)
            acc = acc + jnp.dot(vt, p, preferred_element_type=F32)
            new += [m_new, acc]
        return tuple(new)

    m0 = jnp.full((1, tq), NEG_BIG, F32)
    acc0 = jnp.zeros((VT_ROWS, tq), F32)
    init = (m0, acc0, m0, acc0)

    def trip(j, stats):
        t0 = j * unroll
        qi = t0 // n_kv
        k0 = t0 % n_kv
        stats = tuple(jnp.where(k0 == 0, a, b) for a, b in zip(init, stats))
        for u in range(unroll):
            scores(t0 + u + 1, (u + 1) % 2)
            stats = consume(k0 + u, u % 2, stats)

        @pl.when(k0 + unroll == n_kv)
        def _():
            finish((stats[1], stats[3]), qi)

        return stats

    scores(0, 0)
    lax.fori_loop(0, n_tiles // unroll, trip, init)


def _attention(nmax, q, k, vt, tq, tk, unroll):
    b, h, s, _ = q.shape
    nt = nmax.shape[0] // b
    return pl.pallas_call(
        functools.partial(_attn_kernel, tq=tq, tk=tk, unroll=unroll),
        grid=(b, h // 2),
        in_specs=[
            pl.BlockSpec((nt, 8, LANES), lambda bi, hp: (bi, 0, 0)),
            pl.BlockSpec((1, 2, s, HEAD_PAD), lambda bi, hp: (bi, hp, 0, 0)),
            pl.BlockSpec((1, 2, s, HEAD_PAD), lambda bi, hp: (bi, hp, 0, 0)),
            pl.BlockSpec((1, 2, VT_ROWS, s), lambda bi, hp: (bi, hp, 0, 0)),
        ],
        out_specs=pl.BlockSpec((1, s, 2 * MLA_V), lambda bi, hp: (bi, 0, hp)),
        out_shape=jax.ShapeDtypeStruct((b, s, h * MLA_V), BF16),
        scratch_shapes=[pltpu.VMEM((2, 2, tk, tq), F32)],
        compiler_params=_cparams(("parallel", "parallel")),
        name="attention",
    )(nmax, q, k, vt)


def _merge_kernel(x_ref, y_ref, a_ref, gs_ref, gm_ref, gb_ref, wso_ref, wmo_ref, wo_ref,
                  nw_ref, o_ref):
    y_ssm = jnp.dot(y_ref[...], wso_ref[...], preferred_element_type=F32)
    y_mla = jnp.dot(a_ref[...], wmo_ref[...], preferred_element_type=F32)
    gb = gb_ref[...]
    g_ssm = 1.0 / (1.0 + jnp.exp(-(gs_ref[...].astype(F32) + gb[:, :D_MODEL])))
    g_mla = 1.0 / (1.0 + jnp.exp(-(gm_ref[...].astype(F32) + gb[:, D_MODEL:])))
    mix = (g_ssm * y_ssm + g_mla * y_mla).astype(BF16)
    mixed = jnp.dot(mix, wo_ref[...], preferred_element_type=F32)
    o_ref[...] = x_ref[...] + _rms(mixed, nw_ref[...])


def _merge(x, y, attn, big, gate_b, wso, wmo, wo, nw, tm):
    t, d = x.shape
    const = lambda i: (0, 0)
    g0 = BIG_GATE // D_MODEL
    return pl.pallas_call(
        _merge_kernel,
        grid=(t // tm,),
        in_specs=[
            pl.BlockSpec((tm, d), lambda i: (i, 0)),
            pl.BlockSpec((tm, SSM_D_INNER), lambda i: (i, 0)),
            pl.BlockSpec((tm, d), lambda i: (i, 0)),
            pl.BlockSpec((tm, d), lambda i: (i, g0)),
            pl.BlockSpec((tm, d), lambda i: (i, g0 + 1)),
            pl.BlockSpec((1, 2 * d), const),
            pl.BlockSpec((SSM_D_INNER, d), const),
            pl.BlockSpec((d, d), const),
            pl.BlockSpec((d, d), const),
            pl.BlockSpec((1, d), const),
        ],
        out_specs=pl.BlockSpec((tm, d), lambda i: (i, 0)),
        out_shape=jax.ShapeDtypeStruct((t, d), F32),
        compiler_params=_cparams(("parallel",)),
        name="merge",
    )(x, y, attn, big, big, gate_b, wso, wmo, wo, nw)


def _mlp_kernel(x_ref, g1_ref, wu_ref, wd_ref, g2_ref, o_ref, h_ref, acc_ref):
    j = pl.program_id(1)

    @pl.when(j == 0)
    def _():
        h_ref[...] = _rms(x_ref[...], g1_ref[...]).astype(h_ref.dtype)
        acc_ref[...] = jnp.zeros_like(acc_ref)

    u = jnp.dot(h_ref[...], wu_ref[...], preferred_element_type=F32)
    r = jnp.maximum(u, 0.0)
    acc_ref[...] += jnp.dot((r * r).astype(BF16), wd_ref[...], preferred_element_type=F32)

    @pl.when(j == pl.num_programs(1) - 1)
    def _():
        o_ref[...] = x_ref[...] + _rms(acc_ref[...], g2_ref[...])


def _mlp(x, g1, wu, wd, g2, tm, tf):
    t, d = x.shape
    f = wu.shape[1]
    return pl.pallas_call(
        _mlp_kernel,
        grid=(t // tm, f // tf),
        in_specs=[
            pl.BlockSpec((tm, d), lambda i, j: (i, 0)),
            pl.BlockSpec((1, d), lambda i, j: (0, 0)),
            pl.BlockSpec((d, tf), lambda i, j: (0, j)),
            pl.BlockSpec((tf, d), lambda i, j: (j, 0)),
            pl.BlockSpec((1, d), lambda i, j: (0, 0)),
        ],
        out_specs=pl.BlockSpec((tm, d), lambda i, j: (i, 0)),
        out_shape=jax.ShapeDtypeStruct((t, d), F32),
        scratch_shapes=[pltpu.VMEM((tm, d), BF16), pltpu.VMEM((tm, d), F32)],
        compiler_params=_cparams(("parallel", "arbitrary")),
        name="mlp",
    )(x, g1, wu, wd, g2)


def _group_major(fwd, bwd):
    lead = fwd.shape[:-1]
    f = fwd.reshape(lead + (SSM_N_GROUPS, HEADS_PER_GROUP))
    bw = bwd.reshape(lead + (SSM_N_GROUPS, HEADS_PER_GROUP))
    return jnp.concatenate([f, bw], axis=-1).reshape(lead + (2 * SSM_N_HEADS,))


def _pad_cols(w, n):
    return jnp.pad(w, ((0, 0), (0, n - w.shape[1])))


def _pick(n, prefs):
    for p in prefs:
        if n % p == 0:
            return p
    raise ValueError(f"no tile in {prefs} divides {n}")


def kernel(x, positions, norm_mix_pre, w_in, conv_w, conv_b, dt_bias_fwd, dt_bias_bwd, a_log_fwd, a_log_bwd, d_skip, ssm_norm_w, w_ssm_out, q_a_norm, w_q_b, kv_a_norm, w_kv_b, w_mla_out, gate_b, w_out, norm_mix_post, norm_mlp_pre, w_up, w_down, norm_mlp_post):
    b, s, d = x.shape
    t = b * s
    depth = w_in.shape[0]
    assert d == D_MODEL and s % 512 == 0
    half = MLA_ROPE // 2

    inv_freq = ROPE_BASE ** (-np.arange(0, MLA_ROPE, 2, dtype=np.float32) / MLA_ROPE)
    invf = np.zeros((1, LANES), np.float32)
    invf[0, MLA_NOPE:MLA_NOPE + half] = inv_freq
    invf[0, MLA_NOPE + half:MLA_NOPE + MLA_ROPE] = inv_freq
    sign = np.zeros((1, LANES), np.float32)
    sign[0, MLA_NOPE:MLA_NOPE + half] = -1.0
    sign[0, MLA_NOPE + half:MLA_NOPE + MLA_ROPE] = 1.0
    ones = np.zeros((MLA_N_HEADS, VT_ROWS, 1), np.float32)
    ones[:, VT_ONES_ROW] = 1.0
    ones = ones.reshape(MLA_N_HEADS * VT_ROWS, 1)
    head_of_col = np.arange(MLA_N_HEADS * HEAD_PAD) // HEAD_PAD
    eq = (head_of_col[:, None] == np.arange(LANES)[None, :]).astype(np.float32)
    ek = (head_of_col[:, None] + MLA_N_HEADS == np.arange(LANES)[None, :]).astype(np.float32)
    eq, ek = jnp.asarray(eq, BF16), jnp.asarray(ek, BF16)
    invf, sign, ones = jnp.asarray(invf), jnp.asarray(sign), jnp.asarray(ones)
    perm, sel0, sel1 = (jnp.asarray(m, BF16) for m in _pack_constants())
    pos = positions.astype(F32).reshape(t, 1)

    xt = x.reshape(t, d)
    for l in range(depth):
        wz, wxbc, wdt, wql, wkvl, wkr, wg = jnp.split(
            w_in[l], np.cumsum([SSM_D_INNER, SSM_CONV_DIM, 2 * SSM_N_HEADS, MLA_Q_RANK,
                                MLA_KV_RANK, MLA_ROPE])[:].tolist(), axis=1)
        w_big = jnp.concatenate([wz, wxbc, wg], axis=1).astype(BF16)
        wdt_gm = _group_major(wdt[:, :SSM_N_HEADS], wdt[:, SSM_N_HEADS:])
        wkr_sw = jnp.concatenate([wkr[:, half:], wkr[:, :half]], axis=1)
        zpad = jnp.zeros((d, MLA_NOPE), F32)
        w_small = jnp.concatenate([
            wql, wkvl,
            _pad_cols(jnp.concatenate([zpad, wkr], axis=1), LANES),
            _pad_cols(jnp.concatenate([zpad, wkr_sw], axis=1), LANES),
            _pad_cols(wdt_gm, LANES)], axis=1).astype(BF16)
        dt_bias = _pad_cols(_group_major(dt_bias_fwd[l], dt_bias_bwd[l])[None, :], LANES)
        a_log = _pad_cols(_group_major(a_log_fwd[l], a_log_bwd[l])[None, :], LANES)
        dskip = jnp.repeat(d_skip[l], SSM_HEAD_DIM).reshape(SSM_N_GROUPS, 1, GROUP_W)
        nw_ssm = ssm_norm_w[l].reshape(SSM_N_GROUPS, 1, GROUP_W)

        wq3 = w_q_b[l].reshape(MLA_Q_RANK, MLA_N_HEADS, MLA_QK)
        q_nope, q_pe = wq3[..., :MLA_NOPE], wq3[..., MLA_NOPE:]
        q_pe_sw = jnp.concatenate([q_pe[..., half:], q_pe[..., :half]], axis=-1)
        hz = jnp.zeros((MLA_Q_RANK, MLA_N_HEADS, HEAD_PAD - MLA_QK), F32)
        wq = jnp.concatenate([q_nope, q_pe, hz], axis=-1).reshape(MLA_Q_RANK, -1).astype(BF16)
        wqs = jnp.concatenate([jnp.zeros_like(q_nope), q_pe_sw, hz],
                              axis=-1).reshape(MLA_Q_RANK, -1).astype(BF16)
        wkv3 = w_kv_b[l].reshape(MLA_KV_RANK, MLA_N_HEADS, MLA_NOPE + MLA_V)
        k_nope, v_w = wkv3[..., :MLA_NOPE], wkv3[..., MLA_NOPE:]
        z64 = jnp.zeros_like(k_nope)
        wk = jnp.concatenate([k_nope, z64], axis=-1).reshape(MLA_KV_RANK, -1).astype(BF16)
        wvt = jnp.pad(jnp.transpose(v_w, (1, 2, 0)), ((0, 0), (0, VT_ROWS - MLA_V), (0, 0)))
        wvt = wvt.reshape(MLA_N_HEADS * VT_ROWS, MLA_KV_RANK).astype(BF16)

        tm = _pick(t, (1024, 512))
        gain = norm_mix_pre[l][None, :]
        big = _norm_matmul(xt, gain, w_big, BF16, tm, 1024)
        small = _norm_matmul(xt, gain, w_small, F32, tm, SM_W)

        pack, dtr, cumr = _dt_prep(small, dt_bias, a_log, perm, _pick(t, (1024, 512)))
        big3 = big.reshape(b, s, BIG_W)
        conv = _conv(big3, conv_w[l], conv_b[l][None, :], 512, 512)
        y = _ssd(conv, big3, pack, dtr, cumr, sel0, sel1, dskip, nw_ssm, _pick(s, (1024, 512)))

        q, k, vt, nmax = _mla_prep(small, pos, invf, sign, q_a_norm[l][None, :],
                                   kv_a_norm[l][None, :], wq, wqs, wk, wvt, ones, eq, ek, b, s, 512)
        attn = _attention(nmax, q, k, vt, ATT_TQ, ATT_TK, ATT_UNROLL)

        x1 = _merge(xt, y.reshape(t, SSM_D_INNER), attn.reshape(t, d), big,
                    gate_b[l][None, :], w_ssm_out[l].astype(BF16), w_mla_out[l].astype(BF16),
                    w_out[l].astype(BF16), norm_mix_post[l][None, :], 512)

        xt = _mlp(x1, norm_mlp_pre[l][None, :], w_up[l].astype(BF16), w_down[l].astype(BF16),
                  norm_mlp_post[l][None, :], tm, 1024)
    return xt.reshape(b, s, d)
```

```python
import functools
import math

import numpy as np
import jax
import jax.numpy as jnp
from jax import lax
from jax.experimental import pallas as pl
from jax.experimental.pallas import tpu as pltpu

F32 = jnp.float32
BF16 = jnp.bfloat16

D_MODEL = 1024
SSM_D_INNER = 2048
SSM_HEAD_DIM = 64
SSM_N_HEADS = 32
SSM_N_GROUPS = 8
SSM_D_STATE = 128
SSM_CONV = 5
CHUNK = 128
SSM_CONV_DIM = SSM_D_INNER + 2 * SSM_N_GROUPS * SSM_D_STATE
GROUP_W = SSM_D_INNER // SSM_N_GROUPS
HEADS_PER_GROUP = SSM_N_HEADS // SSM_N_GROUPS
MLA_N_HEADS = 16
MLA_Q_RANK = 256
MLA_KV_RANK = 256
MLA_NOPE = 64
MLA_ROPE = 32
MLA_V = 64
MLA_QK = MLA_NOPE + MLA_ROPE
ROPE_BASE = 10000.0
D_FF = 4 * D_MODEL
EPS = 1e-6
LANES = 128
HEAD_PAD = 128
VT_ROWS = 80
VT_ONES_ROW = MLA_V
LOG2E = 1.4426950408889634
NEG_BIG = -1e30
SCORE_BOUND = 80.0

BIG_Z = 0
BIG_XBC = SSM_D_INNER
BIG_GATE = SSM_D_INNER + SSM_CONV_DIM
BIG_W = BIG_GATE + 2 * D_MODEL
SM_Q = 0
SM_KV = 256
SM_KRA = 512
SM_KRB = 640
SM_DT = 768
SM_W = 896

VMEM_LIMIT = 52 * 1024 * 1024

ATT_TQ = 512
ATT_TK = 1024
ATT_UNROLL = 4
SSD_UNROLL_FWD = 8
SSD_UNROLL_BWD = 8


def _cparams(sem):
    return pltpu.CompilerParams(dimension_semantics=sem, vmem_limit_bytes=VMEM_LIMIT)


def _norm_matmul_kernel(x_ref, g_ref, w_ref, o_ref, h_ref):
    @pl.when(pl.program_id(1) == 0)
    def _():
        x = x_ref[...]
        ms = jnp.mean(x * x, axis=-1, keepdims=True)
        h_ref[...] = (x * lax.rsqrt(ms + EPS) * g_ref[...]).astype(h_ref.dtype)

    o_ref[...] = jnp.dot(h_ref[...], w_ref[...],
                         preferred_element_type=F32).astype(o_ref.dtype)


def _norm_matmul(x, gain, w, out_dtype, tm, tn):
    t, d = x.shape
    n = w.shape[1]
    return pl.pallas_call(
        _norm_matmul_kernel,
        grid=(t // tm, n // tn),
        in_specs=[
            pl.BlockSpec((tm, d), lambda i, j: (i, 0)),
            pl.BlockSpec((1, d), lambda i, j: (0, 0)),
            pl.BlockSpec((d, tn), lambda i, j: (0, j)),
        ],
        out_specs=pl.BlockSpec((tm, tn), lambda i, j: (i, j)),
        out_shape=jax.ShapeDtypeStruct((t, n), out_dtype),
        scratch_shapes=[pltpu.VMEM((tm, d), BF16)],
        compiler_params=_cparams(("parallel", "arbitrary")),
        name="in_proj",
    )(x, gain, w)


def _split3(x):
    hi = x.astype(BF16)
    r = x - hi.astype(F32)
    mid = r.astype(BF16)
    lo = (r - mid.astype(F32)).astype(BF16)
    return [hi, mid, lo]


def _dt_prep_kernel(raw_ref, bias_ref, alog_ref, perm_ref, pack_ref, dtr_ref, cumr_ref):
    tb = raw_ref.shape[0]
    v = raw_ref[...] + bias_ref[...]
    dt = jnp.maximum(v, 0.0) + jnp.log(1.0 + jnp.exp(-jnp.abs(v)))
    la = dt * (-jnp.exp(alog_ref[...]))
    lane = lax.broadcasted_iota(jnp.int32, (CHUNK, LANES), 1)
    is_fwd = (lane % (2 * HEADS_PER_GROUP)) < HEADS_PER_GROUP
    row = lax.broadcasted_iota(jnp.int32, (CHUNK, CHUNK), 0)
    col = lax.broadcasted_iota(jnp.int32, (CHUNK, CHUNK), 1)
    lower = (col <= row).astype(F32)
    upper = (col >= row).astype(F32)
    cums, wgts = [], []
    for c in range(tb // CHUNK):
        la_c = la[c * CHUNK:(c + 1) * CHUNK]
        la_f = jnp.where(is_fwd, la_c, 0.0)
        la_b = jnp.where(is_fwd, 0.0, la_c)
        cum_c = (jnp.dot(lower, la_f, precision=lax.Precision.HIGHEST,
                         preferred_element_type=F32)
                 + jnp.dot(upper, la_b, precision=lax.Precision.HIGHEST,
                           preferred_element_type=F32))
        edge = jnp.where(is_fwd[:1], cum_c[CHUNK - 1:CHUNK], cum_c[0:1])
        cums.append(cum_c)
        wgts.append(dt[c * CHUNK:(c + 1) * CHUNK] * jnp.exp(edge - cum_c))
    cum = jnp.concatenate(cums, axis=0)
    wgt = jnp.concatenate(wgts, axis=0)
    parts = _split3(jnp.exp(cum)) + _split3(wgt) + _split3(cum)
    pack_ref[...] = jnp.dot(jnp.concatenate(parts, axis=1), perm_ref[...],
                            preferred_element_type=F32).astype(pack_ref.dtype)
    dt_t = dt.T
    cum_t = cum.T
    w = 2 * HEADS_PER_GROUP
    for g in range(SSM_N_GROUPS):
        dtr_ref[g] = dt_t[g * w:(g + 1) * w, :]
        cumr_ref[g] = cum_t[g * w:(g + 1) * w, :]


def _dt_prep(small, bias, alog, perm, tb):
    t = small.shape[0]
    w = 2 * HEADS_PER_GROUP
    row_shape = jax.ShapeDtypeStruct((SSM_N_GROUPS, w, t), F32)
    return pl.pallas_call(
        _dt_prep_kernel,
        grid=(t // tb,),
        in_specs=[
            pl.BlockSpec((tb, LANES), lambda i: (i, SM_DT // LANES)),
            pl.BlockSpec((1, LANES), lambda i: (0, 0)),
            pl.BlockSpec((1, LANES), lambda i: (0, 0)),
            pl.BlockSpec(perm.shape, lambda i: (0, 0)),
        ],
        out_specs=[
            pl.BlockSpec((tb, SSM_N_GROUPS * LANES), lambda i: (i, 0)),
            pl.BlockSpec((SSM_N_GROUPS, w, tb), lambda i: (0, 0, i)),
            pl.BlockSpec((SSM_N_GROUPS, w, tb), lambda i: (0, 0, i)),
        ],
        out_shape=[jax.ShapeDtypeStruct((t, SSM_N_GROUPS * LANES), BF16), row_shape, row_shape],
        compiler_params=_cparams(("parallel",)),
        name="dt_prep",
    )(small, bias, alog, perm)


def _pack_constants():
    w = 2 * HEADS_PER_GROUP
    n_parts = 9
    perm = np.zeros((n_parts * LANES, SSM_N_GROUPS * LANES), np.float32)
    for q in range(n_parts):
        for g in range(SSM_N_GROUPS):
            for i in range(w):
                perm[q * LANES + g * w + i, g * LANES + q * w + i] = 1.0

    def expand_cols(first_part, j0):
        m = np.zeros((LANES, GROUP_W), np.float32)
        for p in range(GROUP_W):
            for q in range(3):
                m[(first_part + q) * w + j0 + p // SSM_HEAD_DIM, p] = 1.0
        return m

    def bcast_cols(first_part):
        m = np.zeros((LANES, w * CHUNK), np.float32)
        for i in range(w):
            for q in range(3):
                m[(first_part + q) * w + i, i * CHUNK:(i + 1) * CHUNK] = 1.0
        return m

    h = HEADS_PER_GROUP
    sel0 = np.concatenate([expand_cols(3, 0), expand_cols(0, 0)], axis=1)
    sel1 = np.concatenate([expand_cols(0, 0), expand_cols(0, h), expand_cols(3, h),
                           bcast_cols(6)], axis=1)
    return perm, sel0, sel1


HALO = 16


CONV_WIN = CHUNK + 2 * HALO


CONV_SIDE_TAPS = tuple(k for k in range(SSM_CONV) if k != SSM_CONV // 2)


def _conv_select_matrix():
    m = np.zeros((CHUNK, len(CONV_SIDE_TAPS) * CONV_WIN), np.float32)
    for n, k in enumerate(CONV_SIDE_TAPS):
        for i in range(CHUNK):
            m[i, n * CONV_WIN + HALO + i + k - SSM_CONV // 2] = 1.0
    return m


def _conv_kernel(x_ref, hp_ref, hn_ref, w_ref, b_ref, sel_ref, o_ref):
    si = pl.program_id(1)
    ns = pl.num_programs(1)
    ts = x_ref.shape[1]
    zero = jnp.zeros((HALO, x_ref.shape[2]), x_ref.dtype)
    hp = jnp.where(si > 0, hp_ref[0], zero)
    hn = jnp.where(si < ns - 1, hn_ref[0], zero)
    ext = jnp.concatenate([hp, x_ref[0], hn], axis=0)
    wf = w_ref[...]
    w = wf.astype(BF16)
    w_mid = wf[SSM_CONV // 2:SSM_CONV // 2 + 1, :]
    bias = b_ref[...]
    sel = sel_ref[...]
    for r in range(ts // CHUNK):
        win = ext[r * CHUNK:r * CHUNK + CONV_WIN]
        stacked = jnp.concatenate([win * w[k:k + 1, :] for k in CONV_SIDE_TAPS], axis=0)
        mid = x_ref[0, r * CHUNK:(r + 1) * CHUNK, :].astype(F32) * w_mid
        acc = jnp.dot(sel, stacked, preferred_element_type=F32) + (mid + bias)
        half = 0.5 * acc
        o_ref[0, r * CHUNK:(r + 1) * CHUNK, :] = (half + half * jnp.tanh(half)).astype(o_ref.dtype)


def _conv(big, conv_w, conv_b, ts, tc):
    b, s, _ = big.shape
    sel = jnp.asarray(_conv_select_matrix(), BF16)
    c_total = conv_w.shape[1]
    col0 = BIG_XBC // tc
    nh = ts // HALO
    last_h = s // HALO - 1
    return pl.pallas_call(
        _conv_kernel,
        grid=(b, s // ts, c_total // tc),
        in_specs=[
            pl.BlockSpec((1, ts, tc), lambda bi, si, ci: (bi, si, col0 + ci)),
            pl.BlockSpec((1, HALO, tc),
                         lambda bi, si, ci: (bi, jnp.maximum(si * nh - 1, 0), col0 + ci)),
            pl.BlockSpec((1, HALO, tc),
                         lambda bi, si, ci: (bi, jnp.minimum((si + 1) * nh, last_h), col0 + ci)),
            pl.BlockSpec((SSM_CONV, tc), lambda bi, si, ci: (0, ci)),
            pl.BlockSpec((1, tc), lambda bi, si, ci: (0, ci)),
            pl.BlockSpec(sel.shape, lambda bi, si, ci: (0, 0)),
        ],
        out_specs=pl.BlockSpec((1, ts, tc), lambda bi, si, ci: (bi, si, ci)),
        out_shape=jax.ShapeDtypeStruct((b, s, c_total), BF16),
        compiler_params=_cparams(("parallel", "parallel", "parallel")),
        name="conv_silu",
    )(big, big, big, conv_w, conv_b, sel)


def _ssd_kernel(x_ref, b_ref, c_ref, z_ref, pack_ref, dtr_ref, cumr_ref, sel0_ref, sel1_ref,
                dskip_ref, nw_ref, o_ref, sprev_ref, state_ref):
    phase = pl.program_id(2)
    blk = pl.program_id(3)
    nblk = pl.num_programs(3)
    tb = x_ref.shape[1]
    ncb = tb // CHUNK
    H = HEADS_PER_GROUP
    W = GROUP_W

    @pl.when(blk == 0)
    def _():
        state_ref[...] = jnp.zeros_like(state_ref)

    @pl.when(phase == 0)
    def _():
        def body(i, carry):
            r0 = pl.multiple_of(i * CHUNK, CHUNK)
            x = x_ref[0, pl.ds(r0, CHUNK), :].astype(F32)
            bm = b_ref[0, pl.ds(r0, CHUNK), :]
            e0 = jnp.dot(pack_ref[pl.ds(r0, CHUNK), :], sel0_ref[...],
                         preferred_element_type=F32)
            wa, ef = e0[:, :W], e0[:, W:]
            xa = (x * wa).astype(BF16)
            s_chunk = lax.dot_general(bm, xa, (((0,), (0,)), ((), ())),
                                      preferred_element_type=F32)
            st = state_ref[...]
            sprev_ref[blk * ncb + i] = st
            state_ref[...] = st * ef[CHUNK - 1:CHUNK, :] + s_chunk
            return carry

        lax.fori_loop(0, ncb, body, 0, unroll=SSD_UNROLL_FWD)

    @pl.when(phase == 1)
    def _():
        dskip = dskip_ref[0]
        nw = nw_ref[0]
        li = lax.broadcasted_iota(jnp.int32, (CHUNK, CHUNK), 0)
        si = lax.broadcasted_iota(jnp.int32, (CHUNK, CHUNK), 1)
        lower = li >= si
        upper = li <= si
        head_of_lane = lax.broadcasted_iota(jnp.int32, (CHUNK, W), 1) // SSM_HEAD_DIM

        def body(ii, carry):
            i = ncb - 1 - ii
            r0 = pl.multiple_of(i * CHUNK, CHUNK)
            xb16 = x_ref[0, pl.ds(r0, CHUNK), :]
            x = xb16.astype(F32)
            bm = b_ref[0, pl.ds(r0, CHUNK), :]
            cm = c_ref[0, pl.ds(r0, CHUNK), :]
            z = z_ref[0, pl.ds(r0, CHUNK), :].astype(F32)
            dtr = dtr_ref[0, :, pl.ds(r0, CHUNK)]
            cumr = cumr_ref[0, :, pl.ds(r0, CHUNK)]
            e1 = jnp.dot(pack_ref[pl.ds(r0, CHUNK), :], sel1_ref[...],
                         preferred_element_type=F32)
            ef, eb, wb = e1[:, :W], e1[:, W:2 * W], e1[:, 2 * W:3 * W]
            ccol = e1[:, 3 * W:]

            cb = lax.dot_general(cm, bm, (((1,), (1,)), ((), ())), preferred_element_type=F32)
            ms, xs = [], []
            for j in range(H):
                cf = ccol[:, j * CHUNK:(j + 1) * CHUNK]
                cbw = ccol[:, (H + j) * CHUNK:(H + j + 1) * CHUNK]
                df = jnp.exp(jnp.where(lower, cf - cumr[j:j + 1, :], NEG_BIG))
                db = jnp.exp(jnp.where(upper, cbw - cumr[H + j:H + j + 1, :], NEG_BIG))
                ms.append((cb * (df * dtr[j:j + 1, :] + db * dtr[H + j:H + j + 1, :])).astype(BF16))
                xs.append(jnp.where(head_of_lane == j, xb16, jnp.zeros_like(xb16)))
            y = jnp.dot(jnp.concatenate(ms, axis=1), jnp.concatenate(xs, axis=0),
                        preferred_element_type=F32)

            chunk_id = (nblk - 1 - blk) * ncb + i
            sp_f = sprev_ref[chunk_id].astype(BF16)
            st_b = state_ref[...]
            y = y + jnp.dot(cm, sp_f, preferred_element_type=F32) * ef
            y = y + jnp.dot(cm, st_b.astype(BF16), preferred_element_type=F32) * eb
            y = y + dskip * x

            xw = (x * wb).astype(BF16)
            s_chunk = lax.dot_general(bm, xw, (((0,), (0,)), ((), ())),
                                      preferred_element_type=F32)
            state_ref[...] = st_b * eb[0:1, :] + s_chunk

            yg = y * (z * (1.0 / (1.0 + jnp.exp(-z))))
            msq = jnp.mean(yg * yg, axis=-1, keepdims=True)
            o_ref[0, pl.ds(r0, CHUNK), :] = (yg * lax.rsqrt(msq + EPS) * nw).astype(o_ref.dtype)
            return carry

        lax.fori_loop(0, ncb, body, 0, unroll=SSD_UNROLL_BWD)


def _ssd(conv, big, pack, dtr, cumr, sel0, sel1, dskip, nw, tb):
    b, s, _ = conv.shape
    nblk = s // tb
    w = 2 * HEADS_PER_GROUP
    const = lambda bi, g, p, k: (0, 0)

    def sidx(p, k):
        return jnp.where(p == 0, k, nblk - 1 - k)

    def sidx_late(p, k):
        return jnp.where(p == 0, nblk - 1, nblk - 1 - k)

    b_col0 = SSM_D_INNER // SSM_D_STATE
    c_col0 = b_col0 + SSM_N_GROUPS
    return pl.pallas_call(
        _ssd_kernel,
        grid=(b, SSM_N_GROUPS, 2, nblk),
        in_specs=[
            pl.BlockSpec((1, tb, GROUP_W), lambda bi, g, p, k: (bi, sidx(p, k), g)),
            pl.BlockSpec((1, tb, SSM_D_STATE), lambda bi, g, p, k: (bi, sidx(p, k), b_col0 + g)),
            pl.BlockSpec((1, tb, SSM_D_STATE),
                         lambda bi, g, p, k: (bi, sidx_late(p, k), c_col0 + g)),
            pl.BlockSpec((1, tb, GROUP_W), lambda bi, g, p, k: (bi, sidx_late(p, k), g)),
            pl.BlockSpec((tb, LANES), lambda bi, g, p, k: (bi * nblk + sidx(p, k), g)),
            pl.BlockSpec((1, w, tb), lambda bi, g, p, k: (g, 0, bi * nblk + sidx_late(p, k))),
            pl.BlockSpec((1, w, tb), lambda bi, g, p, k: (g, 0, bi * nblk + sidx_late(p, k))),
            pl.BlockSpec(sel0.shape, const),
            pl.BlockSpec(sel1.shape, const),
            pl.BlockSpec((1, 1, GROUP_W), lambda bi, g, p, k: (g, 0, 0)),
            pl.BlockSpec((1, 1, GROUP_W), lambda bi, g, p, k: (g, 0, 0)),
        ],
        out_specs=pl.BlockSpec((1, tb, GROUP_W), lambda bi, g, p, k: (bi, sidx_late(p, k), g)),
        out_shape=jax.ShapeDtypeStruct((b, s, SSM_D_INNER), BF16),
        scratch_shapes=[
            pltpu.VMEM((s // CHUNK, SSM_D_STATE, GROUP_W), F32),
            pltpu.VMEM((SSM_D_STATE, GROUP_W), F32),
        ],
        compiler_params=_cparams(("parallel", "parallel", "arbitrary", "arbitrary")),
        name="ssd",
    )(conv, conv, conv, big, pack, dtr, cumr, sel0, sel1, dskip, nw)


def _rms(x, g):
    ms = jnp.mean(x * x, axis=-1, keepdims=True)
    return x * lax.rsqrt(ms + EPS) * g


def _mla_prep_kernel(ql_ref, kvl_ref, kra_ref, krb_ref, pos_ref, invf_ref,
                     gq_ref, gkv_ref, wq_ref, wqs_ref, wk_ref, wvt_ref, ones_ref,
                     eq_ref, ek_ref, q_ref, k_ref, vt_ref, nmax_ref):
    tm = ql_ref.shape[0]
    ang = invf_ref[...] * pos_ref[0]
    cos_t, sin_t = jnp.cos(ang), jnp.sin(ang)
    pad = jnp.zeros((HEAD_PAD - MLA_QK, tm), F32)
    cos = jnp.concatenate([jnp.ones((MLA_NOPE, tm), F32), cos_t, cos_t, pad], axis=0).T
    sin = jnp.concatenate([jnp.zeros((MLA_NOPE, tm), F32), -sin_t, sin_t, pad], axis=0).T
    scale = MLA_QK ** -0.5 * LOG2E
    cq = cos * scale
    sq = sin * scale
    qn = _rms(ql_ref[...], gq_ref[...]).astype(BF16)
    qa = jnp.dot(qn, wq_ref[...], preferred_element_type=F32)
    qb = jnp.dot(qn, wqs_ref[...], preferred_element_type=F32)
    kn = _rms(kvl_ref[...], gkv_ref[...]).astype(BF16)
    kk = jnp.dot(kn, wk_ref[...], preferred_element_type=F32)
    vvt = lax.dot_general(wvt_ref[...], kn, (((1,), (1,)), ((), ())),
                          preferred_element_type=F32) + ones_ref[...]
    kpe = kra_ref[...] * cos + krb_ref[...] * sin
    qsq, ksq = [], []
    for h in range(MLA_N_HEADS):
        sl = slice(h * HEAD_PAD, (h + 1) * HEAD_PAD)
        qh = (qa[:, sl] * cq + qb[:, sl] * sq).astype(BF16)
        kh = (kk[:, sl] + kpe).astype(BF16)
        q_ref[0, h] = qh
        k_ref[0, h] = kh
        qsq.append(qh * qh)
        ksq.append(kh * kh)
        vt_ref[0, h] = vvt[h * VT_ROWS:(h + 1) * VT_ROWS, :].astype(vt_ref.dtype)
    nsq = (jnp.dot(jnp.concatenate(qsq, axis=1), eq_ref[...], preferred_element_type=F32)
           + jnp.dot(jnp.concatenate(ksq, axis=1), ek_ref[...], preferred_element_type=F32))
    nmax_ref[0] = jnp.max(nsq.reshape(nsq.shape[0] // 8, 8, LANES), axis=0)


def _mla_prep(small, pos, invf, gq, gkv, wq, wqs, wk, wvt, ones, eq, ek, b, s, tm):
    t = small.shape[0]
    nsb = s // tm
    hw = MLA_N_HEADS * HEAD_PAD
    out = jax.ShapeDtypeStruct((b, MLA_N_HEADS, s, HEAD_PAD), BF16)
    out_vt = jax.ShapeDtypeStruct((b, MLA_N_HEADS, VT_ROWS, s), BF16)
    const = lambda i: (0, 0)
    ospec = pl.BlockSpec((1, MLA_N_HEADS, tm, HEAD_PAD), lambda i: (i // nsb, 0, i % nsb, 0))
    ospec_vt = pl.BlockSpec((1, MLA_N_HEADS, VT_ROWS, tm), lambda i: (i // nsb, 0, 0, i % nsb))
    return pl.pallas_call(
        _mla_prep_kernel,
        grid=(t // tm,),
        in_specs=[
            pl.BlockSpec((tm, MLA_Q_RANK), lambda i: (i, SM_Q // MLA_Q_RANK)),
            pl.BlockSpec((tm, MLA_KV_RANK), lambda i: (i, SM_KV // MLA_KV_RANK)),
            pl.BlockSpec((tm, LANES), lambda i: (i, SM_KRA // LANES)),
            pl.BlockSpec((tm, LANES), lambda i: (i, SM_KRB // LANES)),
            pl.BlockSpec((1, 1, tm), lambda i: (i // nsb, 0, i % nsb)),
            pl.BlockSpec((MLA_ROPE // 2, 1), const),
            pl.BlockSpec((1, MLA_Q_RANK), const),
            pl.BlockSpec((1, MLA_KV_RANK), const),
            pl.BlockSpec((MLA_Q_RANK, hw), const),
            pl.BlockSpec((MLA_Q_RANK, hw), const),
            pl.BlockSpec((MLA_KV_RANK, hw), const),
            pl.BlockSpec((MLA_N_HEADS * VT_ROWS, MLA_KV_RANK), const),
            pl.BlockSpec((MLA_N_HEADS * VT_ROWS, 1), const),
            pl.BlockSpec((hw, LANES), const),
            pl.BlockSpec((hw, LANES), const),
        ],
        out_specs=[ospec, ospec, ospec_vt, pl.BlockSpec((1, 8, LANES), lambda i: (i, 0, 0))],
        out_shape=[out, out, out_vt, jax.ShapeDtypeStruct((t // tm, 8, LANES), F32)],
        compiler_params=_cparams(("parallel",)),
        name="mla_prep",
    )(small, small, small, small, pos, invf, gq, gkv, wq, wqs, wk, wvt, ones, eq, ek)


def _attn_kernel(nmax_ref, q_ref, k_ref, vt_ref, o_ref, s_ref, *, tq, tk, unroll):
    s = k_ref.shape[2]
    n_kv = s // tk
    n_q = s // tq
    n_tiles = n_q * n_kv
    assert n_kv % unroll == 0 and unroll % 2 == 0

    def finish(accs, qi):
        outs = [a[:MLA_V] / a[VT_ONES_ROW:VT_ONES_ROW + 1] for a in accs]
        q0 = pl.multiple_of(qi * tq, tq)
        o_ref[0, pl.ds(q0, tq), :] = jnp.concatenate(outs, axis=0).T.astype(o_ref.dtype)

    hp = pl.program_id(1)
    nmax = jnp.max(nmax_ref[...], axis=(0, 1), keepdims=True)[0]
    lane = lax.broadcasted_iota(jnp.int32, (1, LANES), 1)
    bound_sq = []
    for hh in range(2):
        qn2 = jnp.max(jnp.where(lane == 2 * hp + hh, nmax, 0.0))
        kn2 = jnp.max(jnp.where(lane == MLA_N_HEADS + 2 * hp + hh, nmax, 0.0))
        bound_sq.append(qn2 * kn2)
    small_scores = jnp.maximum(bound_sq[0], bound_sq[1]) <= SCORE_BOUND ** 2

    @pl.when(small_scores)
    def _():
        _attn_pipeline(q_ref, k_ref, vt_ref, s_ref, finish, tq=tq, tk=tk, unroll=unroll,
                       use_max=False)

    @pl.when(jnp.logical_not(small_scores))
    def _():
        _attn_pipeline(q_ref, k_ref, vt_ref, s_ref, finish, tq=tq, tk=tk, unroll=unroll,
                       use_max=True)


def _attn_pipeline(q_ref, k_ref, vt_ref, s_ref, finish, *, tq, tk, unroll, use_max):
    s = k_ref.shape[2]
    n_kv = s // tk
    n_tiles = (s // tq) * n_kv

    def scores(t, slot):
        t = jnp.minimum(t, n_tiles - 1)
        q0 = pl.multiple_of((t // n_kv) * tq, tq)
        r0 = pl.multiple_of((t % n_kv) * tk, tk)
        for hh in range(2):
            s_ref[slot, hh] = lax.dot_general(k_ref[0, hh, pl.ds(r0, tk), :],
                                              q_ref[0, hh, pl.ds(q0, tq), :],
                                              (((1,), (1,)), ((), ())),
                                              preferred_element_type=F32)

    def consume(ki, slot, stats):
        r0 = pl.multiple_of(ki * tk, tk)
        new = []
        for hh in range(2):
            m, acc = stats[2 * hh], stats[2 * hh + 1]
            vt = vt_ref[0, hh, :, pl.ds(r0, tk)]
            if use_max:
                m_new = jnp.maximum(m, jnp.max(s_ref[slot, hh], axis=0, keepdims=True))
                p = jnp.exp2(s_ref[slot, hh] - m_new).astype(BF16)
                acc = acc * jnp.exp2(m - m_new)
            else:
                m_new = m
                p = jnp.exp2(s_ref[slot, hh]).astype(BF16)
            acc = acc + jnp.dot(vt, p, preferred_element_type=F32)
            new += [m_new, acc]
        return tuple(new)

    m0 = jnp.full((1, tq), NEG_BIG, F32)
    acc0 = jnp.zeros((VT_ROWS, tq), F32)
    init = (m0, acc0, m0, acc0)

    def trip(j, stats):
        t0 = j * unroll
        qi = t0 // n_kv
        k0 = t0 % n_kv
        stats = tuple(jnp.where(k0 == 0, a, b) for a, b in zip(init, stats))
        for u in range(unroll):
            scores(t0 + u + 1, (u + 1) % 2)
            stats = consume(k0 + u, u % 2, stats)

        @pl.when(k0 + unroll == n_kv)
        def _():
            finish((stats[1], stats[3]), qi)

        return stats

    scores(0, 0)
    lax.fori_loop(0, n_tiles // unroll, trip, init)


def _attention(nmax, q, k, vt, tq, tk, unroll):
    b, h, s, _ = q.shape
    nt = nmax.shape[0] // b
    return pl.pallas_call(
        functools.partial(_attn_kernel, tq=tq, tk=tk, unroll=unroll),
        grid=(b, h // 2),
        in_specs=[
            pl.BlockSpec((nt, 8, LANES), lambda bi, hp: (bi, 0, 0)),
            pl.BlockSpec((1, 2, s, HEAD_PAD), lambda bi, hp: (bi, hp, 0, 0)),
            pl.BlockSpec((1, 2, s, HEAD_PAD), lambda bi, hp: (bi, hp, 0, 0)),
            pl.BlockSpec((1, 2, VT_ROWS, s), lambda bi, hp: (bi, hp, 0, 0)),
        ],
        out_specs=pl.BlockSpec((1, s, 2 * MLA_V), lambda bi, hp: (bi, 0, hp)),
        out_shape=jax.ShapeDtypeStruct((b, s, h * MLA_V), BF16),
        scratch_shapes=[pltpu.VMEM((2, 2, tk, tq), F32)],
        compiler_params=_cparams(("parallel", "parallel")),
        name="attention",
    )(nmax, q, k, vt)


def _merge_kernel(x_ref, y_ref, a_ref, gs_ref, gm_ref, gb_ref, wso_ref, wmo_ref, wo_ref,
                  nw_ref, o_ref):
    y_ssm = jnp.dot(y_ref[...], wso_ref[...], preferred_element_type=F32)
    y_mla = jnp.dot(a_ref[...], wmo_ref[...], preferred_element_type=F32)
    gb = gb_ref[...]
    g_ssm = 1.0 / (1.0 + jnp.exp(-(gs_ref[...].astype(F32) + gb[:, :D_MODEL])))
    g_mla = 1.0 / (1.0 + jnp.exp(-(gm_ref[...].astype(F32) + gb[:, D_MODEL:])))
    mix = (g_ssm * y_ssm + g_mla * y_mla).astype(BF16)
    mixed = jnp.dot(mix, wo_ref[...], preferred_element_type=F32)
    o_ref[...] = x_ref[...] + _rms(mixed, nw_ref[...])


def _merge(x, y, attn, big, gate_b, wso, wmo, wo, nw, tm):
    t, d = x.shape
    const = lambda i: (0, 0)
    g0 = BIG_GATE // D_MODEL
    return pl.pallas_call(
        _merge_kernel,
        grid=(t // tm,),
        in_specs=[
            pl.BlockSpec((tm, d), lambda i: (i, 0)),
            pl.BlockSpec((tm, SSM_D_INNER), lambda i: (i, 0)),
            pl.BlockSpec((tm, d), lambda i: (i, 0)),
            pl.BlockSpec((tm, d), lambda i: (i, g0)),
            pl.BlockSpec((tm, d), lambda i: (i, g0 + 1)),
            pl.BlockSpec((1, 2 * d), const),
            pl.BlockSpec((SSM_D_INNER, d), const),
            pl.BlockSpec((d, d), const),
            pl.BlockSpec((d, d), const),
            pl.BlockSpec((1, d), const),
        ],
        out_specs=pl.BlockSpec((tm, d), lambda i: (i, 0)),
        out_shape=jax.ShapeDtypeStruct((t, d), F32),
        compiler_params=_cparams(("parallel",)),
        name="merge",
    )(x, y, attn, big, big, gate_b, wso, wmo, wo, nw)


def _mlp_kernel(x_ref, g1_ref, wu_ref, wd_ref, g2_ref, o_ref, h_ref, acc_ref):
    j = pl.program_id(1)

    @pl.when(j == 0)
    def _():
        h_ref[...] = _rms(x_ref[...], g1_ref[...]).astype(h_ref.dtype)
        acc_ref[...] = jnp.zeros_like(acc_ref)

    u = jnp.dot(h_ref[...], wu_ref[...], preferred_element_type=F32)
    r = jnp.maximum(u, 0.0)
    acc_ref[...] += jnp.dot((r * r).astype(BF16), wd_ref[...], preferred_element_type=F32)

    @pl.when(j == pl.num_programs(1) - 1)
    def _():
        o_ref[...] = x_ref[...] + _rms(acc_ref[...], g2_ref[...])


def _mlp(x, g1, wu, wd, g2, tm, tf):
    t, d = x.shape
    f = wu.shape[1]
    return pl.pallas_call(
        _mlp_kernel,
        grid=(t // tm, f // tf),
        in_specs=[
            pl.BlockSpec((tm, d), lambda i, j: (i, 0)),
            pl.BlockSpec((1, d), lambda i, j: (0, 0)),
            pl.BlockSpec((d, tf), lambda i, j: (0, j)),
            pl.BlockSpec((tf, d), lambda i, j: (j, 0)),
            pl.BlockSpec((1, d), lambda i, j: (0, 0)),
        ],
        out_specs=pl.BlockSpec((tm, d), lambda i, j: (i, 0)),
        out_shape=jax.ShapeDtypeStruct((t, d), F32),
        scratch_shapes=[pltpu.VMEM((tm, d), BF16), pltpu.VMEM((tm, d), F32)],
        compiler_params=_cparams(("parallel", "arbitrary")),
        name="mlp",
    )(x, g1, wu, wd, g2)


def _group_major(fwd, bwd):
    lead = fwd.shape[:-1]
    f = fwd.reshape(lead + (SSM_N_GROUPS, HEADS_PER_GROUP))
    bw = bwd.reshape(lead + (SSM_N_GROUPS, HEADS_PER_GROUP))
    return jnp.concatenate([f, bw], axis=-1).reshape(lead + (2 * SSM_N_HEADS,))


def _pad_cols(w, n):
    return jnp.pad(w, ((0, 0), (0, n - w.shape[1])))


def _pick(n, prefs):
    for p in prefs:
        if n % p == 0:
            return p
    raise ValueError(f"no tile in {prefs} divides {n}")


def kernel(x, positions, norm_mix_pre, w_in, conv_w, conv_b, dt_bias_fwd, dt_bias_bwd, a_log_fwd, a_log_bwd, d_skip, ssm_norm_w, w_ssm_out, q_a_norm, w_q_b, kv_a_norm, w_kv_b, w_mla_out, gate_b, w_out, norm_mix_post, norm_mlp_pre, w_up, w_down, norm_mlp_post):
    b, s, d = x.shape
    t = b * s
    depth = w_in.shape[0]
    assert d == D_MODEL and s % 512 == 0
    half = MLA_ROPE // 2

    inv_freq = ROPE_BASE ** (-np.arange(0, MLA_ROPE, 2, dtype=np.float32) / MLA_ROPE)
    invf = inv_freq.reshape(half, 1)
    ones = np.zeros((MLA_N_HEADS, VT_ROWS, 1), np.float32)
    ones[:, VT_ONES_ROW] = 1.0
    ones = ones.reshape(MLA_N_HEADS * VT_ROWS, 1)
    head_of_col = np.arange(MLA_N_HEADS * HEAD_PAD) // HEAD_PAD
    eq = (head_of_col[:, None] == np.arange(LANES)[None, :]).astype(np.float32)
    ek = (head_of_col[:, None] + MLA_N_HEADS == np.arange(LANES)[None, :]).astype(np.float32)
    eq, ek = jnp.asarray(eq, BF16), jnp.asarray(ek, BF16)
    invf, ones = jnp.asarray(invf), jnp.asarray(ones)
    perm, sel0, sel1 = (jnp.asarray(m, BF16) for m in _pack_constants())
    pos = positions.astype(F32).reshape(b, 1, s)

    xt = x.reshape(t, d)
    for l in range(depth):
        wz, wxbc, wdt, wql, wkvl, wkr, wg = jnp.split(
            w_in[l], np.cumsum([SSM_D_INNER, SSM_CONV_DIM, 2 * SSM_N_HEADS, MLA_Q_RANK,
                                MLA_KV_RANK, MLA_ROPE])[:].tolist(), axis=1)
        w_big = jnp.concatenate([wz, wxbc, wg], axis=1).astype(BF16)
        wdt_gm = _group_major(wdt[:, :SSM_N_HEADS], wdt[:, SSM_N_HEADS:])
        wkr_sw = jnp.concatenate([wkr[:, half:], wkr[:, :half]], axis=1)
        zpad = jnp.zeros((d, MLA_NOPE), F32)
        w_small = jnp.concatenate([
            wql, wkvl,
            _pad_cols(jnp.concatenate([zpad, wkr], axis=1), LANES),
            _pad_cols(jnp.concatenate([zpad, wkr_sw], axis=1), LANES),
            _pad_cols(wdt_gm, LANES)], axis=1).astype(BF16)
        dt_bias = _pad_cols(_group_major(dt_bias_fwd[l], dt_bias_bwd[l])[None, :], LANES)
        a_log = _pad_cols(_group_major(a_log_fwd[l], a_log_bwd[l])[None, :], LANES)
        dskip = jnp.repeat(d_skip[l], SSM_HEAD_DIM).reshape(SSM_N_GROUPS, 1, GROUP_W)
        nw_ssm = ssm_norm_w[l].reshape(SSM_N_GROUPS, 1, GROUP_W)

        wq3 = w_q_b[l].reshape(MLA_Q_RANK, MLA_N_HEADS, MLA_QK)
        q_nope, q_pe = wq3[..., :MLA_NOPE], wq3[..., MLA_NOPE:]
        q_pe_sw = jnp.concatenate([q_pe[..., half:], q_pe[..., :half]], axis=-1)
        hz = jnp.zeros((MLA_Q_RANK, MLA_N_HEADS, HEAD_PAD - MLA_QK), F32)
        wq = jnp.concatenate([q_nope, q_pe, hz], axis=-1).reshape(MLA_Q_RANK, -1).astype(BF16)
        wqs = jnp.concatenate([jnp.zeros_like(q_nope), q_pe_sw, hz],
                              axis=-1).reshape(MLA_Q_RANK, -1).astype(BF16)
        wkv3 = w_kv_b[l].reshape(MLA_KV_RANK, MLA_N_HEADS, MLA_NOPE + MLA_V)
        k_nope, v_w = wkv3[..., :MLA_NOPE], wkv3[..., MLA_NOPE:]
        z64 = jnp.zeros_like(k_nope)
        wk = jnp.concatenate([k_nope, z64], axis=-1).reshape(MLA_KV_RANK, -1).astype(BF16)
        wvt = jnp.pad(jnp.transpose(v_w, (1, 2, 0)), ((0, 0), (0, VT_ROWS - MLA_V), (0, 0)))
        wvt = wvt.reshape(MLA_N_HEADS * VT_ROWS, MLA_KV_RANK).astype(BF16)

        tm = _pick(t, (1024, 512))
        gain = norm_mix_pre[l][None, :]
        big = _norm_matmul(xt, gain, w_big, BF16, tm, 1024)
        small = _norm_matmul(xt, gain, w_small, F32, tm, SM_W)

        pack, dtr, cumr = _dt_prep(small, dt_bias, a_log, perm, _pick(t, (1024, 512)))
        big3 = big.reshape(b, s, BIG_W)
        conv = _conv(big3, conv_w[l], conv_b[l][None, :], _pick(s, (1024, 512)), 512)
        y = _ssd(conv, big3, pack, dtr, cumr, sel0, sel1, dskip, nw_ssm, _pick(s, (1024, 512)))

        q, k, vt, nmax = _mla_prep(small, pos, invf, q_a_norm[l][None, :],
                                   kv_a_norm[l][None, :], wq, wqs, wk, wvt, ones, eq, ek, b, s, 512)
        attn = _attention(nmax, q, k, vt, ATT_TQ, ATT_TK, ATT_UNROLL)

        x1 = _merge(xt, y.reshape(t, SSM_D_INNER), attn.reshape(t, d), big,
                    gate_b[l][None, :], w_ssm_out[l].astype(BF16), w_mla_out[l].astype(BF16),
                    w_out[l].astype(BF16), norm_mix_post[l][None, :], 512)

        xt = _mlp(x1, norm_mlp_pre[l][None, :], w_up[l].astype(BF16), w_down[l].astype(BF16),
                  norm_mlp_post[l][None, :], tm, 1024)
    return xt.reshape(b, s, d)
```

```python
import functools
import math

import numpy as np
import jax
import jax.numpy as jnp
from jax import lax
from jax.experimental import pallas as pl
from jax.experimental.pallas import tpu as pltpu

F32 = jnp.float32
BF16 = jnp.bfloat16

D_MODEL = 1024
SSM_D_INNER = 2048
SSM_HEAD_DIM = 64
SSM_N_HEADS = 32
SSM_N_GROUPS = 8
SSM_D_STATE = 128
SSM_CONV = 5
CHUNK = 128
SSM_CONV_DIM = SSM_D_INNER + 2 * SSM_N_GROUPS * SSM_D_STATE
GROUP_W = SSM_D_INNER // SSM_N_GROUPS
HEADS_PER_GROUP = SSM_N_HEADS // SSM_N_GROUPS
MLA_N_HEADS = 16
MLA_Q_RANK = 256
MLA_KV_RANK = 256
MLA_NOPE = 64
MLA_ROPE = 32
MLA_V = 64
MLA_QK = MLA_NOPE + MLA_ROPE
ROPE_BASE = 10000.0
D_FF = 4 * D_MODEL
EPS = 1e-6
LANES = 128
HEAD_PAD = 128
VT_ROWS = 80
VT_ONES_ROW = MLA_V
LOG2E = 1.4426950408889634
NEG_BIG = -1e30
SCORE_BOUND = 80.0

BIG_Z = 0
BIG_XBC = SSM_D_INNER
BIG_GATE = SSM_D_INNER + SSM_CONV_DIM
BIG_W = BIG_GATE + 2 * D_MODEL
SM_Q = 0
SM_KV = 256
SM_KRA = 512
SM_KRB = 640
SM_DT = 768
SM_W = 896

VMEM_LIMIT = 52 * 1024 * 1024

ATT_TQ = 512
ATT_TK = 1024
ATT_UNROLL = 8
SSD_UNROLL_FWD = 8
SSD_UNROLL_BWD = 8


def _cparams(sem):
    return pltpu.CompilerParams(dimension_semantics=sem, vmem_limit_bytes=VMEM_LIMIT)


def _in_proj_kernel(x_ref, g_ref, w_ref, ws_ref, o_ref, os_ref, h_ref):
    j = pl.program_id(1)

    @pl.when(j == 0)
    def _():
        x = x_ref[...]
        ms = jnp.mean(x * x, axis=-1, keepdims=True)
        h_ref[...] = (x * lax.rsqrt(ms + EPS) * g_ref[...]).astype(h_ref.dtype)

    o_ref[...] = jnp.dot(h_ref[...], w_ref[...],
                         preferred_element_type=F32).astype(o_ref.dtype)

    @pl.when(j == pl.num_programs(1) - 1)
    def _():
        os_ref[...] = jnp.dot(h_ref[...], ws_ref[...], preferred_element_type=F32)


def _in_proj(x, gain, w_big, w_small, tm, tn):
    t, d = x.shape
    n = w_big.shape[1]
    ns = w_small.shape[1]
    return pl.pallas_call(
        _in_proj_kernel,
        grid=(t // tm, n // tn),
        in_specs=[
            pl.BlockSpec((tm, d), lambda i, j: (i, 0)),
            pl.BlockSpec((1, d), lambda i, j: (0, 0)),
            pl.BlockSpec((d, tn), lambda i, j: (0, j)),
            pl.BlockSpec((d, ns), lambda i, j: (0, 0)),
        ],
        out_specs=[pl.BlockSpec((tm, tn), lambda i, j: (i, j)),
                   pl.BlockSpec((tm, ns), lambda i, j: (i, 0))],
        out_shape=[jax.ShapeDtypeStruct((t, n), BF16), jax.ShapeDtypeStruct((t, ns), F32)],
        scratch_shapes=[pltpu.VMEM((tm, d), BF16)],
        compiler_params=_cparams(("parallel", "arbitrary")),
        name="in_proj",
    )(x, gain, w_big, w_small)


def _split3(x):
    hi = x.astype(BF16)
    r = x - hi.astype(F32)
    mid = r.astype(BF16)
    lo = (r - mid.astype(F32)).astype(BF16)
    return [hi, mid, lo]


def _dt_prep_kernel(raw_ref, bias_ref, alog_ref, perm_ref, pack_ref, dtr_ref, cumr_ref):
    tb = raw_ref.shape[0]
    v = raw_ref[...] + bias_ref[...]
    dt = jnp.maximum(v, 0.0) + jnp.log(1.0 + jnp.exp(-jnp.abs(v)))
    la = dt * (-jnp.exp(alog_ref[...]))
    lane = lax.broadcasted_iota(jnp.int32, (CHUNK, LANES), 1)
    is_fwd = (lane % (2 * HEADS_PER_GROUP)) < HEADS_PER_GROUP
    row = lax.broadcasted_iota(jnp.int32, (CHUNK, CHUNK), 0)
    col = lax.broadcasted_iota(jnp.int32, (CHUNK, CHUNK), 1)
    lower = (col <= row).astype(F32)
    upper = (col >= row).astype(F32)
    cums, wgts = [], []
    for c in range(tb // CHUNK):
        la_c = la[c * CHUNK:(c + 1) * CHUNK]
        la_f = jnp.where(is_fwd, la_c, 0.0)
        la_b = jnp.where(is_fwd, 0.0, la_c)
        cum_c = (jnp.dot(lower, la_f, precision=lax.Precision.HIGHEST,
                         preferred_element_type=F32)
                 + jnp.dot(upper, la_b, precision=lax.Precision.HIGHEST,
                           preferred_element_type=F32))
        edge = jnp.where(is_fwd[:1], cum_c[CHUNK - 1:CHUNK], cum_c[0:1])
        cums.append(cum_c)
        wgts.append(dt[c * CHUNK:(c + 1) * CHUNK] * jnp.exp(edge - cum_c))
    cum = jnp.concatenate(cums, axis=0)
    wgt = jnp.concatenate(wgts, axis=0)
    parts = _split3(jnp.exp(cum)) + _split3(wgt) + _split3(cum)
    pack_ref[...] = jnp.dot(jnp.concatenate(parts, axis=1), perm_ref[...],
                            preferred_element_type=F32).astype(pack_ref.dtype)
    dt_t = dt.T
    cum_t = cum.T
    w = 2 * HEADS_PER_GROUP
    for g in range(SSM_N_GROUPS):
        dtr_ref[g] = dt_t[g * w:(g + 1) * w, :]
        cumr_ref[g] = cum_t[g * w:(g + 1) * w, :]


def _dt_prep(small, bias, alog, perm, tb):
    t = small.shape[0]
    w = 2 * HEADS_PER_GROUP
    row_shape = jax.ShapeDtypeStruct((SSM_N_GROUPS, w, t), F32)
    return pl.pallas_call(
        _dt_prep_kernel,
        grid=(t // tb,),
        in_specs=[
            pl.BlockSpec((tb, LANES), lambda i: (i, SM_DT // LANES)),
            pl.BlockSpec((1, LANES), lambda i: (0, 0)),
            pl.BlockSpec((1, LANES), lambda i: (0, 0)),
            pl.BlockSpec(perm.shape, lambda i: (0, 0)),
        ],
        out_specs=[
            pl.BlockSpec((tb, SSM_N_GROUPS * LANES), lambda i: (i, 0)),
            pl.BlockSpec((SSM_N_GROUPS, w, tb), lambda i: (0, 0, i)),
            pl.BlockSpec((SSM_N_GROUPS, w, tb), lambda i: (0, 0, i)),
        ],
        out_shape=[jax.ShapeDtypeStruct((t, SSM_N_GROUPS * LANES), BF16), row_shape, row_shape],
        compiler_params=_cparams(("parallel",)),
        name="dt_prep",
    )(small, bias, alog, perm)


def _pack_constants():
    w = 2 * HEADS_PER_GROUP
    n_parts = 9
    perm = np.zeros((n_parts * LANES, SSM_N_GROUPS * LANES), np.float32)
    for q in range(n_parts):
        for g in range(SSM_N_GROUPS):
            for i in range(w):
                perm[q * LANES + g * w + i, g * LANES + q * w + i] = 1.0

    def expand_cols(first_part, j0):
        m = np.zeros((LANES, GROUP_W), np.float32)
        for p in range(GROUP_W):
            for q in range(3):
                m[(first_part + q) * w + j0 + p // SSM_HEAD_DIM, p] = 1.0
        return m

    def bcast_cols(first_part):
        m = np.zeros((LANES, w * CHUNK), np.float32)
        for i in range(w):
            for q in range(3):
                m[(first_part + q) * w + i, i * CHUNK:(i + 1) * CHUNK] = 1.0
        return m

    h = HEADS_PER_GROUP
    sel0 = np.concatenate([expand_cols(3, 0), expand_cols(0, 0)], axis=1)
    sel1 = np.concatenate([expand_cols(0, 0), expand_cols(0, h), expand_cols(3, h),
                           bcast_cols(6)], axis=1)
    return perm, sel0, sel1


HALO = 16


CONV_WIN = CHUNK + 2 * HALO


CONV_SIDE_TAPS = tuple(k for k in range(SSM_CONV) if k != SSM_CONV // 2)


def _conv_select_matrix():
    m = np.zeros((CHUNK, len(CONV_SIDE_TAPS) * CONV_WIN), np.float32)
    for n, k in enumerate(CONV_SIDE_TAPS):
        for i in range(CHUNK):
            m[i, n * CONV_WIN + HALO + i + k - SSM_CONV // 2] = 1.0
    return m


def _conv_kernel(x_ref, hp_ref, hn_ref, w_ref, b_ref, sel_ref, o_ref):
    si = pl.program_id(1)
    ns = pl.num_programs(1)
    ts = x_ref.shape[1]
    zero = jnp.zeros((HALO, x_ref.shape[2]), x_ref.dtype)
    hp = jnp.where(si > 0, hp_ref[0], zero)
    hn = jnp.where(si < ns - 1, hn_ref[0], zero)
    ext = jnp.concatenate([hp, x_ref[0], hn], axis=0)
    wf = w_ref[...]
    w = wf.astype(BF16)
    w_mid = wf[SSM_CONV // 2:SSM_CONV // 2 + 1, :]
    bias = b_ref[...]
    sel = sel_ref[...]
    for r in range(ts // CHUNK):
        win = ext[r * CHUNK:r * CHUNK + CONV_WIN]
        stacked = jnp.concatenate([win * w[k:k + 1, :] for k in CONV_SIDE_TAPS], axis=0)
        mid = x_ref[0, r * CHUNK:(r + 1) * CHUNK, :].astype(F32) * w_mid
        acc = jnp.dot(sel, stacked, preferred_element_type=F32) + (mid + bias)
        half = 0.5 * acc
        o_ref[0, r * CHUNK:(r + 1) * CHUNK, :] = (half + half * jnp.tanh(half)).astype(o_ref.dtype)


def _conv(big, conv_w, conv_b, ts, tc):
    b, s, _ = big.shape
    sel = jnp.asarray(_conv_select_matrix(), BF16)
    c_total = conv_w.shape[1]
    col0 = BIG_XBC // tc
    nh = ts // HALO
    last_h = s // HALO - 1
    return pl.pallas_call(
        _conv_kernel,
        grid=(b, s // ts, c_total // tc),
        in_specs=[
            pl.BlockSpec((1, ts, tc), lambda bi, si, ci: (bi, si, col0 + ci)),
            pl.BlockSpec((1, HALO, tc),
                         lambda bi, si, ci: (bi, jnp.maximum(si * nh - 1, 0), col0 + ci)),
            pl.BlockSpec((1, HALO, tc),
                         lambda bi, si, ci: (bi, jnp.minimum((si + 1) * nh, last_h), col0 + ci)),
            pl.BlockSpec((SSM_CONV, tc), lambda bi, si, ci: (0, ci)),
            pl.BlockSpec((1, tc), lambda bi, si, ci: (0, ci)),
            pl.BlockSpec(sel.shape, lambda bi, si, ci: (0, 0)),
        ],
        out_specs=pl.BlockSpec((1, ts, tc), lambda bi, si, ci: (bi, si, ci)),
        out_shape=jax.ShapeDtypeStruct((b, s, c_total), BF16),
        compiler_params=_cparams(("parallel", "parallel", "parallel")),
        name="conv_silu",
    )(big, big, big, conv_w, conv_b, sel)


def _ssd_kernel(x_ref, b_ref, c_ref, z_ref, pack_ref, dtr_ref, cumr_ref, sel0_ref, sel1_ref,
                dskip_ref, nw_ref, o_ref, sprev_ref, state_ref):
    phase = pl.program_id(2)
    blk = pl.program_id(3)
    nblk = pl.num_programs(3)
    tb = x_ref.shape[1]
    ncb = tb // CHUNK
    H = HEADS_PER_GROUP
    W = GROUP_W

    @pl.when(blk == 0)
    def _():
        state_ref[...] = jnp.zeros_like(state_ref)

    @pl.when(phase == 0)
    def _():
        def body(i, carry):
            r0 = pl.multiple_of(i * CHUNK, CHUNK)
            x = x_ref[0, pl.ds(r0, CHUNK), :].astype(F32)
            bm = b_ref[0, pl.ds(r0, CHUNK), :]
            e0 = jnp.dot(pack_ref[pl.ds(r0, CHUNK), :], sel0_ref[...],
                         preferred_element_type=F32)
            wa, ef = e0[:, :W], e0[:, W:]
            xa = (x * wa).astype(BF16)
            s_chunk = lax.dot_general(bm, xa, (((0,), (0,)), ((), ())),
                                      preferred_element_type=F32)
            st = state_ref[...]
            sprev_ref[blk * ncb + i] = st
            state_ref[...] = st * ef[CHUNK - 1:CHUNK, :] + s_chunk
            return carry

        lax.fori_loop(0, ncb, body, 0, unroll=SSD_UNROLL_FWD)

    @pl.when(phase == 1)
    def _():
        dskip = dskip_ref[0]
        nw = nw_ref[0]
        li = lax.broadcasted_iota(jnp.int32, (CHUNK, CHUNK), 0)
        si = lax.broadcasted_iota(jnp.int32, (CHUNK, CHUNK), 1)
        lower = li >= si
        upper = li <= si
        head_of_lane = lax.broadcasted_iota(jnp.int32, (CHUNK, W), 1) // SSM_HEAD_DIM

        def body(ii, carry):
            i = ncb - 1 - ii
            r0 = pl.multiple_of(i * CHUNK, CHUNK)
            xb16 = x_ref[0, pl.ds(r0, CHUNK), :]
            x = xb16.astype(F32)
            bm = b_ref[0, pl.ds(r0, CHUNK), :]
            cm = c_ref[0, pl.ds(r0, CHUNK), :]
            z = z_ref[0, pl.ds(r0, CHUNK), :].astype(F32)
            dtr = dtr_ref[0, :, pl.ds(r0, CHUNK)]
            cumr = cumr_ref[0, :, pl.ds(r0, CHUNK)]
            e1 = jnp.dot(pack_ref[pl.ds(r0, CHUNK), :], sel1_ref[...],
                         preferred_element_type=F32)
            ef, eb, wb = e1[:, :W], e1[:, W:2 * W], e1[:, 2 * W:3 * W]
            ccol = e1[:, 3 * W:]

            cb = lax.dot_general(cm, bm, (((1,), (1,)), ((), ())), preferred_element_type=F32)
            ms, xs = [], []
            for j in range(H):
                cf = ccol[:, j * CHUNK:(j + 1) * CHUNK]
                cbw = ccol[:, (H + j) * CHUNK:(H + j + 1) * CHUNK]
                df = jnp.exp(jnp.where(lower, cf - cumr[j:j + 1, :], NEG_BIG))
                db = jnp.exp(jnp.where(upper, cbw - cumr[H + j:H + j + 1, :], NEG_BIG))
                ms.append((cb * (df * dtr[j:j + 1, :] + db * dtr[H + j:H + j + 1, :])).astype(BF16))
                xs.append(jnp.where(head_of_lane == j, xb16, jnp.zeros_like(xb16)))
            y = jnp.dot(jnp.concatenate(ms, axis=1), jnp.concatenate(xs, axis=0),
                        preferred_element_type=F32)

            chunk_id = (nblk - 1 - blk) * ncb + i
            sp_f = sprev_ref[chunk_id].astype(BF16)
            st_b = state_ref[...]
            y = y + jnp.dot(cm, sp_f, preferred_element_type=F32) * ef
            y = y + jnp.dot(cm, st_b.astype(BF16), preferred_element_type=F32) * eb
            y = y + dskip * x

            xw = (x * wb).astype(BF16)
            s_chunk = lax.dot_general(bm, xw, (((0,), (0,)), ((), ())),
                                      preferred_element_type=F32)
            state_ref[...] = st_b * eb[0:1, :] + s_chunk

            yg = y * (z * (1.0 / (1.0 + jnp.exp(-z))))
            msq = jnp.mean(yg * yg, axis=-1, keepdims=True)
            o_ref[0, pl.ds(r0, CHUNK), :] = (yg * lax.rsqrt(msq + EPS) * nw).astype(o_ref.dtype)
            return carry

        lax.fori_loop(0, ncb, body, 0, unroll=SSD_UNROLL_BWD)


def _ssd(conv, big, pack, dtr, cumr, sel0, sel1, dskip, nw, tb):
    b, s, _ = conv.shape
    nblk = s // tb
    w = 2 * HEADS_PER_GROUP
    const = lambda bi, g, p, k: (0, 0)

    def sidx(p, k):
        return jnp.where(p == 0, k, nblk - 1 - k)

    def sidx_late(p, k):
        return jnp.where(p == 0, nblk - 1, nblk - 1 - k)

    b_col0 = SSM_D_INNER // SSM_D_STATE
    c_col0 = b_col0 + SSM_N_GROUPS
    return pl.pallas_call(
        _ssd_kernel,
        grid=(b, SSM_N_GROUPS, 2, nblk),
        in_specs=[
            pl.BlockSpec((1, tb, GROUP_W), lambda bi, g, p, k: (bi, sidx(p, k), g)),
            pl.BlockSpec((1, tb, SSM_D_STATE), lambda bi, g, p, k: (bi, sidx(p, k), b_col0 + g)),
            pl.BlockSpec((1, tb, SSM_D_STATE),
                         lambda bi, g, p, k: (bi, sidx_late(p, k), c_col0 + g)),
            pl.BlockSpec((1, tb, GROUP_W), lambda bi, g, p, k: (bi, sidx_late(p, k), g)),
            pl.BlockSpec((tb, LANES), lambda bi, g, p, k: (bi * nblk + sidx(p, k), g)),
            pl.BlockSpec((1, w, tb), lambda bi, g, p, k: (g, 0, bi * nblk + sidx_late(p, k))),
            pl.BlockSpec((1, w, tb), lambda bi, g, p, k: (g, 0, bi * nblk + sidx_late(p, k))),
            pl.BlockSpec(sel0.shape, const),
            pl.BlockSpec(sel1.shape, const),
            pl.BlockSpec((1, 1, GROUP_W), lambda bi, g, p, k: (g, 0, 0)),
            pl.BlockSpec((1, 1, GROUP_W), lambda bi, g, p, k: (g, 0, 0)),
        ],
        out_specs=pl.BlockSpec((1, tb, GROUP_W), lambda bi, g, p, k: (bi, sidx_late(p, k), g)),
        out_shape=jax.ShapeDtypeStruct((b, s, SSM_D_INNER), BF16),
        scratch_shapes=[
            pltpu.VMEM((s // CHUNK, SSM_D_STATE, GROUP_W), F32),
            pltpu.VMEM((SSM_D_STATE, GROUP_W), F32),
        ],
        compiler_params=_cparams(("parallel", "parallel", "arbitrary", "arbitrary")),
        name="ssd",
    )(conv, conv, conv, big, pack, dtr, cumr, sel0, sel1, dskip, nw)


def _rms(x, g):
    ms = jnp.mean(x * x, axis=-1, keepdims=True)
    return x * lax.rsqrt(ms + EPS) * g


def _mla_prep_kernel(ql_ref, kvl_ref, kra_ref, krb_ref, pos_ref, invf_ref,
                     gq_ref, gkv_ref, wq_ref, wqs_ref, wk_ref, wvt_ref, ones_ref,
                     eq_ref, ek_ref, q_ref, k_ref, vt_ref, nmax_ref):
    tm = ql_ref.shape[0]
    ang = invf_ref[...] * pos_ref[0]
    cos_t, sin_t = jnp.cos(ang), jnp.sin(ang)
    pad = jnp.zeros((HEAD_PAD - MLA_QK, tm), F32)
    cos = jnp.concatenate([jnp.ones((MLA_NOPE, tm), F32), cos_t, cos_t, pad], axis=0).T
    sin = jnp.concatenate([jnp.zeros((MLA_NOPE, tm), F32), -sin_t, sin_t, pad], axis=0).T
    scale = MLA_QK ** -0.5 * LOG2E
    cq = cos * scale
    sq = sin * scale
    qn = _rms(ql_ref[...], gq_ref[...]).astype(BF16)
    qa = jnp.dot(qn, wq_ref[...], preferred_element_type=F32)
    qb = jnp.dot(qn, wqs_ref[...], preferred_element_type=F32)
    kn = _rms(kvl_ref[...], gkv_ref[...]).astype(BF16)
    kk = jnp.dot(kn, wk_ref[...], preferred_element_type=F32)
    vvt = lax.dot_general(wvt_ref[...], kn, (((1,), (1,)), ((), ())),
                          preferred_element_type=F32) + ones_ref[...]
    kpe = kra_ref[...] * cos + krb_ref[...] * sin
    qsq, ksq = [], []
    for h in range(MLA_N_HEADS):
        sl = slice(h * HEAD_PAD, (h + 1) * HEAD_PAD)
        qh = (qa[:, sl] * cq + qb[:, sl] * sq).astype(BF16)
        kh = (kk[:, sl] + kpe).astype(BF16)
        q_ref[0, h] = qh
        k_ref[0, h] = kh
        qsq.append(qh * qh)
        ksq.append(kh * kh)
        vt_ref[0, h] = vvt[h * VT_ROWS:(h + 1) * VT_ROWS, :].astype(vt_ref.dtype)
    nsq = (jnp.dot(jnp.concatenate(qsq, axis=1), eq_ref[...], preferred_element_type=F32)
           + jnp.dot(jnp.concatenate(ksq, axis=1), ek_ref[...], preferred_element_type=F32))
    nmax_ref[0] = jnp.max(nsq.reshape(nsq.shape[0] // 8, 8, LANES), axis=0)


def _mla_prep(small, pos, invf, gq, gkv, wq, wqs, wk, wvt, ones, eq, ek, b, s, tm):
    t = small.shape[0]
    nsb = s // tm
    hw = MLA_N_HEADS * HEAD_PAD
    out = jax.ShapeDtypeStruct((b, MLA_N_HEADS, s, HEAD_PAD), BF16)
    out_vt = jax.ShapeDtypeStruct((b, MLA_N_HEADS, VT_ROWS, s), BF16)
    const = lambda i: (0, 0)
    ospec = pl.BlockSpec((1, MLA_N_HEADS, tm, HEAD_PAD), lambda i: (i // nsb, 0, i % nsb, 0))
    ospec_vt = pl.BlockSpec((1, MLA_N_HEADS, VT_ROWS, tm), lambda i: (i // nsb, 0, 0, i % nsb))
    return pl.pallas_call(
        _mla_prep_kernel,
        grid=(t // tm,),
        in_specs=[
            pl.BlockSpec((tm, MLA_Q_RANK), lambda i: (i, SM_Q // MLA_Q_RANK)),
            pl.BlockSpec((tm, MLA_KV_RANK), lambda i: (i, SM_KV // MLA_KV_RANK)),
            pl.BlockSpec((tm, LANES), lambda i: (i, SM_KRA // LANES)),
            pl.BlockSpec((tm, LANES), lambda i: (i, SM_KRB // LANES)),
            pl.BlockSpec((1, 1, tm), lambda i: (i // nsb, 0, i % nsb)),
            pl.BlockSpec((MLA_ROPE // 2, 1), const),
            pl.BlockSpec((1, MLA_Q_RANK), const),
            pl.BlockSpec((1, MLA_KV_RANK), const),
            pl.BlockSpec((MLA_Q_RANK, hw), const),
            pl.BlockSpec((MLA_Q_RANK, hw), const),
            pl.BlockSpec((MLA_KV_RANK, hw), const),
            pl.BlockSpec((MLA_N_HEADS * VT_ROWS, MLA_KV_RANK), const),
            pl.BlockSpec((MLA_N_HEADS * VT_ROWS, 1), const),
            pl.BlockSpec((hw, LANES), const),
            pl.BlockSpec((hw, LANES), const),
        ],
        out_specs=[ospec, ospec, ospec_vt, pl.BlockSpec((1, 8, LANES), lambda i: (i, 0, 0))],
        out_shape=[out, out, out_vt, jax.ShapeDtypeStruct((t // tm, 8, LANES), F32)],
        compiler_params=_cparams(("parallel",)),
        name="mla_prep",
    )(small, small, small, small, pos, invf, gq, gkv, wq, wqs, wk, wvt, ones, eq, ek)


def _attn_kernel(nmax_ref, q_ref, k_ref, vt_ref, o_ref, s_ref, *, tq, tk, unroll):
    s = k_ref.shape[2]
    n_kv = s // tk
    n_q = s // tq
    n_tiles = n_q * n_kv
    assert n_kv % unroll == 0 and unroll % 2 == 0

    def finish(accs, qi):
        outs = [a[:MLA_V] / a[VT_ONES_ROW:VT_ONES_ROW + 1] for a in accs]
        q0 = pl.multiple_of(qi * tq, tq)
        o_ref[0, pl.ds(q0, tq), :] = jnp.concatenate(outs, axis=0).T.astype(o_ref.dtype)

    hp = pl.program_id(1)
    nmax = jnp.max(nmax_ref[...], axis=(0, 1), keepdims=True)[0]
    lane = lax.broadcasted_iota(jnp.int32, (1, LANES), 1)
    bound_sq = []
    for hh in range(2):
        qn2 = jnp.max(jnp.where(lane == 2 * hp + hh, nmax, 0.0))
        kn2 = jnp.max(jnp.where(lane == MLA_N_HEADS + 2 * hp + hh, nmax, 0.0))
        bound_sq.append(qn2 * kn2)
    small_scores = jnp.maximum(bound_sq[0], bound_sq[1]) <= SCORE_BOUND ** 2

    @pl.when(small_scores)
    def _():
        _attn_pipeline(q_ref, k_ref, vt_ref, s_ref, finish, tq=tq, tk=tk, unroll=unroll,
                       use_max=False)

    @pl.when(jnp.logical_not(small_scores))
    def _():
        _attn_pipeline(q_ref, k_ref, vt_ref, s_ref, finish, tq=tq, tk=tk, unroll=unroll,
                       use_max=True)


def _attn_pipeline(q_ref, k_ref, vt_ref, s_ref, finish, *, tq, tk, unroll, use_max):
    s = k_ref.shape[2]
    n_kv = s // tk
    n_tiles = (s // tq) * n_kv

    def scores(t, slot):
        t = jnp.minimum(t, n_tiles - 1)
        q0 = pl.multiple_of((t // n_kv) * tq, tq)
        r0 = pl.multiple_of((t % n_kv) * tk, tk)
        for hh in range(2):
            s_ref[slot, hh] = lax.dot_general(k_ref[0, hh, pl.ds(r0, tk), :],
                                              q_ref[0, hh, pl.ds(q0, tq), :],
                                              (((1,), (1,)), ((), ())),
                                              preferred_element_type=F32)

    def consume(ki, slot, stats):
        r0 = pl.multiple_of(ki * tk, tk)
        new = []
        for hh in range(2):
            m, acc = stats[2 * hh], stats[2 * hh + 1]
            vt = vt_ref[0, hh, :, pl.ds(r0, tk)]
            if use_max:
                m_new = jnp.maximum(m, jnp.max(s_ref[slot, hh], axis=0, keepdims=True))
                p = jnp.exp2(s_ref[slot, hh] - m_new).astype(BF16)
                acc = acc * jnp.exp2(m - m_new)
            else:
                m_new = m
                p = jnp.exp2(s_ref[slot, hh]).astype(BF16)
            acc = acc + jnp.dot(vt, p, preferred_element_type=F32)
            new += [m_new, acc]
        return tuple(new)

    m0 = jnp.full((1, tq), NEG_BIG, F32)
    acc0 = jnp.zeros((VT_ROWS, tq), F32)
    init = (m0, acc0, m0, acc0)

    def trip(j, stats):
        t0 = j * unroll
        qi = t0 // n_kv
        k0 = t0 % n_kv
        stats = tuple(jnp.where(k0 == 0, a, b) for a, b in zip(init, stats))
        for u in range(unroll):
            scores(t0 + u + 1, (u + 1) % 2)
            stats = consume(k0 + u, u % 2, stats)

        @pl.when(k0 + unroll == n_kv)
        def _():
            finish((stats[1], stats[3]), qi)

        return stats

    scores(0, 0)
    lax.fori_loop(0, n_tiles // unroll, trip, init)


def _attention(nmax, q, k, vt, tq, tk, unroll):
    b, h, s, _ = q.shape
    nt = nmax.shape[0] // b
    return pl.pallas_call(
        functools.partial(_attn_kernel, tq=tq, tk=tk, unroll=unroll),
        grid=(b, h // 2),
        in_specs=[
            pl.BlockSpec((nt, 8, LANES), lambda bi, hp: (bi, 0, 0)),
            pl.BlockSpec((1, 2, s, HEAD_PAD), lambda bi, hp: (bi, hp, 0, 0)),
            pl.BlockSpec((1, 2, s, HEAD_PAD), lambda bi, hp: (bi, hp, 0, 0)),
            pl.BlockSpec((1, 2, VT_ROWS, s), lambda bi, hp: (bi, hp, 0, 0)),
        ],
        out_specs=pl.BlockSpec((1, s, 2 * MLA_V), lambda bi, hp: (bi, 0, hp)),
        out_shape=jax.ShapeDtypeStruct((b, s, h * MLA_V), BF16),
        scratch_shapes=[pltpu.VMEM((2, 2, tk, tq), F32)],
        compiler_params=_cparams(("parallel", "parallel")),
        name="attention",
    )(nmax, q, k, vt)


def _merge_kernel(x_ref, y_ref, a_ref, gs_ref, gm_ref, gb_ref, wso_ref, wmo_ref, wo_ref,
                  nw_ref, o_ref):
    y_ssm = jnp.dot(y_ref[...], wso_ref[...], preferred_element_type=F32)
    y_mla = jnp.dot(a_ref[...], wmo_ref[...], preferred_element_type=F32)
    gb = gb_ref[...]
    g_ssm = 1.0 / (1.0 + jnp.exp(-(gs_ref[...].astype(F32) + gb[:, :D_MODEL])))
    g_mla = 1.0 / (1.0 + jnp.exp(-(gm_ref[...].astype(F32) + gb[:, D_MODEL:])))
    mix = (g_ssm * y_ssm + g_mla * y_mla).astype(BF16)
    mixed = jnp.dot(mix, wo_ref[...], preferred_element_type=F32)
    o_ref[...] = x_ref[...] + _rms(mixed, nw_ref[...])


def _merge(x, y, attn, big, gate_b, wso, wmo, wo, nw, tm):
    t, d = x.shape
    const = lambda i: (0, 0)
    g0 = BIG_GATE // D_MODEL
    return pl.pallas_call(
        _merge_kernel,
        grid=(t // tm,),
        in_specs=[
            pl.BlockSpec((tm, d), lambda i: (i, 0)),
            pl.BlockSpec((tm, SSM_D_INNER), lambda i: (i, 0)),
            pl.BlockSpec((tm, d), lambda i: (i, 0)),
            pl.BlockSpec((tm, d), lambda i: (i, g0)),
            pl.BlockSpec((tm, d), lambda i: (i, g0 + 1)),
            pl.BlockSpec((1, 2 * d), const),
            pl.BlockSpec((SSM_D_INNER, d), const),
            pl.BlockSpec((d, d), const),
            pl.BlockSpec((d, d), const),
            pl.BlockSpec((1, d), const),
        ],
        out_specs=pl.BlockSpec((tm, d), lambda i: (i, 0)),
        out_shape=jax.ShapeDtypeStruct((t, d), F32),
        compiler_params=_cparams(("parallel",)),
        name="merge",
    )(x, y, attn, big, big, gate_b, wso, wmo, wo, nw)


def _mlp_kernel(x_ref, g1_ref, wu_ref, wd_ref, g2_ref, o_ref, h_ref, acc_ref):
    j = pl.program_id(1)

    @pl.when(j == 0)
    def _():
        h_ref[...] = _rms(x_ref[...], g1_ref[...]).astype(h_ref.dtype)
        acc_ref[...] = jnp.zeros_like(acc_ref)

    u = jnp.dot(h_ref[...], wu_ref[...], preferred_element_type=F32)
    r = jnp.maximum(u, 0.0)
    acc_ref[...] += jnp.dot((r * r).astype(BF16), wd_ref[...], preferred_element_type=F32)

    @pl.when(j == pl.num_programs(1) - 1)
    def _():
        o_ref[...] = x_ref[...] + _rms(acc_ref[...], g2_ref[...])


def _mlp(x, g1, wu, wd, g2, tm, tf):
    t, d = x.shape
    f = wu.shape[1]
    return pl.pallas_call(
        _mlp_kernel,
        grid=(t // tm, f // tf),
        in_specs=[
            pl.BlockSpec((tm, d), lambda i, j: (i, 0)),
            pl.BlockSpec((1, d), lambda i, j: (0, 0)),
            pl.BlockSpec((d, tf), lambda i, j: (0, j)),
            pl.BlockSpec((tf, d), lambda i, j: (j, 0)),
            pl.BlockSpec((1, d), lambda i, j: (0, 0)),
        ],
        out_specs=pl.BlockSpec((tm, d), lambda i, j: (i, 0)),
        out_shape=jax.ShapeDtypeStruct((t, d), F32),
        scratch_shapes=[pltpu.VMEM((tm, d), BF16), pltpu.VMEM((tm, d), F32)],
        compiler_params=_cparams(("parallel", "arbitrary")),
        name="mlp",
    )(x, g1, wu, wd, g2)


def _group_major(fwd, bwd):
    lead = fwd.shape[:-1]
    f = fwd.reshape(lead + (SSM_N_GROUPS, HEADS_PER_GROUP))
    bw = bwd.reshape(lead + (SSM_N_GROUPS, HEADS_PER_GROUP))
    return jnp.concatenate([f, bw], axis=-1).reshape(lead + (2 * SSM_N_HEADS,))


def _pad_cols(w, n):
    return jnp.pad(w, ((0, 0), (0, n - w.shape[1])))


def _pick(n, prefs):
    for p in prefs:
        if n % p == 0:
            return p
    raise ValueError(f"no tile in {prefs} divides {n}")


def kernel(x, positions, norm_mix_pre, w_in, conv_w, conv_b, dt_bias_fwd, dt_bias_bwd, a_log_fwd, a_log_bwd, d_skip, ssm_norm_w, w_ssm_out, q_a_norm, w_q_b, kv_a_norm, w_kv_b, w_mla_out, gate_b, w_out, norm_mix_post, norm_mlp_pre, w_up, w_down, norm_mlp_post):
    b, s, d = x.shape
    t = b * s
    depth = w_in.shape[0]
    assert d == D_MODEL and s % 512 == 0
    half = MLA_ROPE // 2

    inv_freq = ROPE_BASE ** (-np.arange(0, MLA_ROPE, 2, dtype=np.float32) / MLA_ROPE)
    invf = inv_freq.reshape(half, 1)
    ones = np.zeros((MLA_N_HEADS, VT_ROWS, 1), np.float32)
    ones[:, VT_ONES_ROW] = 1.0
    ones = ones.reshape(MLA_N_HEADS * VT_ROWS, 1)
    head_of_col = np.arange(MLA_N_HEADS * HEAD_PAD) // HEAD_PAD
    eq = (head_of_col[:, None] == np.arange(LANES)[None, :]).astype(np.float32)
    ek = (head_of_col[:, None] + MLA_N_HEADS == np.arange(LANES)[None, :]).astype(np.float32)
    eq, ek = jnp.asarray(eq, BF16), jnp.asarray(ek, BF16)
    invf, ones = jnp.asarray(invf), jnp.asarray(ones)
    perm, sel0, sel1 = (jnp.asarray(m, BF16) for m in _pack_constants())
    pos = positions.astype(F32).reshape(b, 1, s)

    xt = x.reshape(t, d)
    for l in range(depth):
        wz, wxbc, wdt, wql, wkvl, wkr, wg = jnp.split(
            w_in[l], np.cumsum([SSM_D_INNER, SSM_CONV_DIM, 2 * SSM_N_HEADS, MLA_Q_RANK,
                                MLA_KV_RANK, MLA_ROPE])[:].tolist(), axis=1)
        w_big = jnp.concatenate([wz, wxbc, wg], axis=1).astype(BF16)
        wdt_gm = _group_major(wdt[:, :SSM_N_HEADS], wdt[:, SSM_N_HEADS:])
        wkr_sw = jnp.concatenate([wkr[:, half:], wkr[:, :half]], axis=1)
        zpad = jnp.zeros((d, MLA_NOPE), F32)
        w_small = jnp.concatenate([
            wql, wkvl,
            _pad_cols(jnp.concatenate([zpad, wkr], axis=1), LANES),
            _pad_cols(jnp.concatenate([zpad, wkr_sw], axis=1), LANES),
            _pad_cols(wdt_gm, LANES)], axis=1).astype(BF16)
        dt_bias = _pad_cols(_group_major(dt_bias_fwd[l], dt_bias_bwd[l])[None, :], LANES)
        a_log = _pad_cols(_group_major(a_log_fwd[l], a_log_bwd[l])[None, :], LANES)
        dskip = jnp.repeat(d_skip[l], SSM_HEAD_DIM).reshape(SSM_N_GROUPS, 1, GROUP_W)
        nw_ssm = ssm_norm_w[l].reshape(SSM_N_GROUPS, 1, GROUP_W)

        wq3 = w_q_b[l].reshape(MLA_Q_RANK, MLA_N_HEADS, MLA_QK)
        q_nope, q_pe = wq3[..., :MLA_NOPE], wq3[..., MLA_NOPE:]
        q_pe_sw = jnp.concatenate([q_pe[..., half:], q_pe[..., :half]], axis=-1)
        hz = jnp.zeros((MLA_Q_RANK, MLA_N_HEADS, HEAD_PAD - MLA_QK), F32)
        wq = jnp.concatenate([q_nope, q_pe, hz], axis=-1).reshape(MLA_Q_RANK, -1).astype(BF16)
        wqs = jnp.concatenate([jnp.zeros_like(q_nope), q_pe_sw, hz],
                              axis=-1).reshape(MLA_Q_RANK, -1).astype(BF16)
        wkv3 = w_kv_b[l].reshape(MLA_KV_RANK, MLA_N_HEADS, MLA_NOPE + MLA_V)
        k_nope, v_w = wkv3[..., :MLA_NOPE], wkv3[..., MLA_NOPE:]
        z64 = jnp.zeros_like(k_nope)
        wk = jnp.concatenate([k_nope, z64], axis=-1).reshape(MLA_KV_RANK, -1).astype(BF16)
        wvt = jnp.pad(jnp.transpose(v_w, (1, 2, 0)), ((0, 0), (0, VT_ROWS - MLA_V), (0, 0)))
        wvt = wvt.reshape(MLA_N_HEADS * VT_ROWS, MLA_KV_RANK).astype(BF16)

        tm = _pick(t, (1024, 512))
        gain = norm_mix_pre[l][None, :]
        big, small = _in_proj(xt, gain, w_big, w_small, tm, 1024)

        pack, dtr, cumr = _dt_prep(small, dt_bias, a_log, perm, _pick(t, (1024, 512)))
        big3 = big.reshape(b, s, BIG_W)
        conv = _conv(big3, conv_w[l], conv_b[l][None, :], _pick(s, (1024, 512)), 512)
        y = _ssd(conv, big3, pack, dtr, cumr, sel0, sel1, dskip, nw_ssm,
                 _pick(s, (2048, 1024)))

        q, k, vt, nmax = _mla_prep(small, pos, invf, q_a_norm[l][None, :],
                                   kv_a_norm[l][None, :], wq, wqs, wk, wvt, ones, eq, ek, b, s,
                                   _pick(s, (1024, 512)))
        attn = _attention(nmax, q, k, vt, ATT_TQ, ATT_TK, ATT_UNROLL)

        x1 = _merge(xt, y.reshape(t, SSM_D_INNER), attn.reshape(t, d), big,
                    gate_b[l][None, :], w_ssm_out[l].astype(BF16), w_mla_out[l].astype(BF16),
                    w_out[l].astype(BF16), norm_mix_post[l][None, :], 512)

        xt = _mlp(x1, norm_mlp_pre[l][None, :], w_up[l].astype(BF16), w_down[l].astype(BF16),
                  norm_mlp_post[l][None, :], tm, 1024)
    return xt.reshape(b, s, d)
```

```python
import functools
import math

import numpy as np
import jax
import jax.numpy as jnp
from jax import lax
from jax.experimental import pallas as pl
from jax.experimental.pallas import tpu as pltpu

F32 = jnp.float32
BF16 = jnp.bfloat16

D_MODEL = 1024
SSM_D_INNER = 2048
SSM_HEAD_DIM = 64
SSM_N_HEADS = 32
SSM_N_GROUPS = 8
SSM_D_STATE = 128
SSM_CONV = 5
CHUNK = 128
SSM_CONV_DIM = SSM_D_INNER + 2 * SSM_N_GROUPS * SSM_D_STATE
GROUP_W = SSM_D_INNER // SSM_N_GROUPS
HEADS_PER_GROUP = SSM_N_HEADS // SSM_N_GROUPS
MLA_N_HEADS = 16
MLA_Q_RANK = 256
MLA_KV_RANK = 256
MLA_NOPE = 64
MLA_ROPE = 32
MLA_V = 64
MLA_QK = MLA_NOPE + MLA_ROPE
ROPE_BASE = 10000.0
D_FF = 4 * D_MODEL
EPS = 1e-6
LANES = 128
HEAD_PAD = 128
VT_ROWS = 80
VT_ONES_ROW = MLA_V
LOG2E = 1.4426950408889634
NEG_BIG = -1e30
SCORE_BOUND = 80.0

BIG_Z = 0
BIG_XBC = SSM_D_INNER
BIG_GATE = SSM_D_INNER + SSM_CONV_DIM
BIG_W = BIG_GATE + 2 * D_MODEL
SM_Q = 0
SM_KV = 256
SM_KRA = 512
SM_KRB = 640
SM_DT = 768
SM_W = 896

VMEM_LIMIT = 52 * 1024 * 1024

ATT_TQ = 512
ATT_TK = 1024
ATT_UNROLL = 8
SSD_UNROLL_FWD = 8
SSD_UNROLL_BWD = 8


def _cparams(sem):
    return pltpu.CompilerParams(dimension_semantics=sem, vmem_limit_bytes=VMEM_LIMIT)


def _in_proj_kernel(x_ref, g_ref, w_ref, ws_ref, o_ref, os_ref, h_ref):
    j = pl.program_id(1)

    @pl.when(j == 0)
    def _():
        x = x_ref[...]
        ms = jnp.mean(x * x, axis=-1, keepdims=True)
        h_ref[...] = (x * lax.rsqrt(ms + EPS) * g_ref[...]).astype(h_ref.dtype)

    o_ref[...] = jnp.dot(h_ref[...], w_ref[...],
                         preferred_element_type=F32).astype(o_ref.dtype)

    @pl.when(j == pl.num_programs(1) - 1)
    def _():
        os_ref[...] = jnp.dot(h_ref[...], ws_ref[...], preferred_element_type=F32)


def _in_proj(x, gain, w_big, w_small, tm, tn):
    t, d = x.shape
    n = w_big.shape[1]
    ns = w_small.shape[1]
    return pl.pallas_call(
        _in_proj_kernel,
        grid=(t // tm, n // tn),
        in_specs=[
            pl.BlockSpec((tm, d), lambda i, j: (i, 0)),
            pl.BlockSpec((1, d), lambda i, j: (0, 0)),
            pl.BlockSpec((d, tn), lambda i, j: (0, j)),
            pl.BlockSpec((d, ns), lambda i, j: (0, 0)),
        ],
        out_specs=[pl.BlockSpec((tm, tn), lambda i, j: (i, j)),
                   pl.BlockSpec((tm, ns), lambda i, j: (i, 0))],
        out_shape=[jax.ShapeDtypeStruct((t, n), BF16), jax.ShapeDtypeStruct((t, ns), F32)],
        scratch_shapes=[pltpu.VMEM((tm, d), BF16)],
        compiler_params=_cparams(("parallel", "arbitrary")),
        name="in_proj",
    )(x, gain, w_big, w_small)


def _split3(x):
    hi = x.astype(BF16)
    r = x - hi.astype(F32)
    mid = r.astype(BF16)
    lo = (r - mid.astype(F32)).astype(BF16)
    return [hi, mid, lo]


def _dt_prep_kernel(raw_ref, bias_ref, alog_ref, perm_ref, pack_ref, dtr_ref, cumr_ref):
    tb = raw_ref.shape[0]
    v = raw_ref[...] + bias_ref[...]
    dt = jnp.maximum(v, 0.0) + jnp.log(1.0 + jnp.exp(-jnp.abs(v)))
    la = dt * (-jnp.exp(alog_ref[...]))
    lane = lax.broadcasted_iota(jnp.int32, (CHUNK, LANES), 1)
    is_fwd = (lane % (2 * HEADS_PER_GROUP)) < HEADS_PER_GROUP
    row = lax.broadcasted_iota(jnp.int32, (CHUNK, CHUNK), 0)
    col = lax.broadcasted_iota(jnp.int32, (CHUNK, CHUNK), 1)
    lower = (col <= row).astype(F32)
    upper = (col >= row).astype(F32)
    cums, wgts = [], []
    for c in range(tb // CHUNK):
        la_c = la[c * CHUNK:(c + 1) * CHUNK]
        la_f = jnp.where(is_fwd, la_c, 0.0)
        la_b = jnp.where(is_fwd, 0.0, la_c)
        cum_c = (jnp.dot(lower, la_f, precision=lax.Precision.HIGHEST,
                         preferred_element_type=F32)
                 + jnp.dot(upper, la_b, precision=lax.Precision.HIGHEST,
                           preferred_element_type=F32))
        edge = jnp.where(is_fwd[:1], cum_c[CHUNK - 1:CHUNK], cum_c[0:1])
        cums.append(cum_c)
        wgts.append(dt[c * CHUNK:(c + 1) * CHUNK] * jnp.exp(edge - cum_c))
    cum = jnp.concatenate(cums, axis=0)
    wgt = jnp.concatenate(wgts, axis=0)
    parts = _split3(jnp.exp(cum)) + _split3(wgt) + _split3(cum)
    pack_ref[...] = jnp.dot(jnp.concatenate(parts, axis=1), perm_ref[...],
                            preferred_element_type=F32).astype(pack_ref.dtype)
    dt_t = dt.T
    cum_t = cum.T
    w = 2 * HEADS_PER_GROUP
    for g in range(SSM_N_GROUPS):
        dtr_ref[g] = dt_t[g * w:(g + 1) * w, :]
        cumr_ref[g] = cum_t[g * w:(g + 1) * w, :]


def _dt_prep(small, bias, alog, perm, tb):
    t = small.shape[0]
    w = 2 * HEADS_PER_GROUP
    row_shape = jax.ShapeDtypeStruct((SSM_N_GROUPS, w, t), F32)
    return pl.pallas_call(
        _dt_prep_kernel,
        grid=(t // tb,),
        in_specs=[
            pl.BlockSpec((tb, LANES), lambda i: (i, SM_DT // LANES)),
            pl.BlockSpec((1, LANES), lambda i: (0, 0)),
            pl.BlockSpec((1, LANES), lambda i: (0, 0)),
            pl.BlockSpec(perm.shape, lambda i: (0, 0)),
        ],
        out_specs=[
            pl.BlockSpec((tb, SSM_N_GROUPS * LANES), lambda i: (i, 0)),
            pl.BlockSpec((SSM_N_GROUPS, w, tb), lambda i: (0, 0, i)),
            pl.BlockSpec((SSM_N_GROUPS, w, tb), lambda i: (0, 0, i)),
        ],
        out_shape=[jax.ShapeDtypeStruct((t, SSM_N_GROUPS * LANES), BF16), row_shape, row_shape],
        compiler_params=_cparams(("parallel",)),
        name="dt_prep",
    )(small, bias, alog, perm)


def _pack_constants():
    w = 2 * HEADS_PER_GROUP
    n_parts = 9
    perm = np.zeros((n_parts * LANES, SSM_N_GROUPS * LANES), np.float32)
    for q in range(n_parts):
        for g in range(SSM_N_GROUPS):
            for i in range(w):
                perm[q * LANES + g * w + i, g * LANES + q * w + i] = 1.0

    def expand_cols(first_part, j0):
        m = np.zeros((LANES, GROUP_W), np.float32)
        for p in range(GROUP_W):
            for q in range(3):
                m[(first_part + q) * w + j0 + p // SSM_HEAD_DIM, p] = 1.0
        return m

    def bcast_cols(first_part):
        m = np.zeros((LANES, w * CHUNK), np.float32)
        for i in range(w):
            for q in range(3):
                m[(first_part + q) * w + i, i * CHUNK:(i + 1) * CHUNK] = 1.0
        return m

    h = HEADS_PER_GROUP
    sel0 = np.concatenate([expand_cols(3, 0), expand_cols(0, 0)], axis=1)
    sel1 = np.concatenate([expand_cols(0, 0), expand_cols(0, h), expand_cols(3, h),
                           bcast_cols(6)], axis=1)
    return perm, sel0, sel1


HALO = 16


CONV_WIN = CHUNK + 2 * HALO


CONV_SIDE_TAPS = tuple(k for k in range(SSM_CONV) if k != SSM_CONV // 2)


def _conv_select_matrix():
    m = np.zeros((CHUNK, len(CONV_SIDE_TAPS) * CONV_WIN), np.float32)
    for n, k in enumerate(CONV_SIDE_TAPS):
        for i in range(CHUNK):
            m[i, n * CONV_WIN + HALO + i + k - SSM_CONV // 2] = 1.0
    return m


def _conv_kernel(x_ref, hp_ref, hn_ref, w_ref, b_ref, sel_ref, o_ref):
    si = pl.program_id(1)
    ns = pl.num_programs(1)
    ts = x_ref.shape[1]
    zero = jnp.zeros((HALO, x_ref.shape[2]), x_ref.dtype)
    hp = jnp.where(si > 0, hp_ref[0], zero)
    hn = jnp.where(si < ns - 1, hn_ref[0], zero)
    ext = jnp.concatenate([hp, x_ref[0], hn], axis=0)
    wf = w_ref[...]
    w = wf.astype(BF16)
    w_mid = wf[SSM_CONV // 2:SSM_CONV // 2 + 1, :]
    bias = b_ref[...]
    sel = sel_ref[...]
    for r in range(ts // CHUNK):
        win = ext[r * CHUNK:r * CHUNK + CONV_WIN]
        stacked = jnp.concatenate([win * w[k:k + 1, :] for k in CONV_SIDE_TAPS], axis=0)
        mid = x_ref[0, r * CHUNK:(r + 1) * CHUNK, :].astype(F32) * w_mid
        acc = jnp.dot(sel, stacked, preferred_element_type=F32) + (mid + bias)
        half = 0.5 * acc
        o_ref[0, r * CHUNK:(r + 1) * CHUNK, :] = (half + half * jnp.tanh(half)).astype(o_ref.dtype)


def _conv(big, conv_w, conv_b, ts, tc):
    b, s, _ = big.shape
    sel = jnp.asarray(_conv_select_matrix(), BF16)
    c_total = conv_w.shape[1]
    col0 = BIG_XBC // tc
    nh = ts // HALO
    last_h = s // HALO - 1
    return pl.pallas_call(
        _conv_kernel,
        grid=(b, s // ts, c_total // tc),
        in_specs=[
            pl.BlockSpec((1, ts, tc), lambda bi, si, ci: (bi, si, col0 + ci)),
            pl.BlockSpec((1, HALO, tc),
                         lambda bi, si, ci: (bi, jnp.maximum(si * nh - 1, 0), col0 + ci)),
            pl.BlockSpec((1, HALO, tc),
                         lambda bi, si, ci: (bi, jnp.minimum((si + 1) * nh, last_h), col0 + ci)),
            pl.BlockSpec((SSM_CONV, tc), lambda bi, si, ci: (0, ci)),
            pl.BlockSpec((1, tc), lambda bi, si, ci: (0, ci)),
            pl.BlockSpec(sel.shape, lambda bi, si, ci: (0, 0)),
        ],
        out_specs=pl.BlockSpec((1, ts, tc), lambda bi, si, ci: (bi, si, ci)),
        out_shape=jax.ShapeDtypeStruct((b, s, c_total), BF16),
        compiler_params=_cparams(("parallel", "parallel", "parallel")),
        name="conv_silu",
    )(big, big, big, conv_w, conv_b, sel)


def _ssd_kernel(x_ref, b_ref, c_ref, z_ref, pack_ref, dtr_ref, cumr_ref, sel0_ref, sel1_ref,
                dskip_ref, nw_ref, o_ref, sprev_ref, state_ref):
    phase = pl.program_id(2)
    blk = pl.program_id(3)
    nblk = pl.num_programs(3)
    tb = x_ref.shape[1]
    ncb = tb // CHUNK
    H = HEADS_PER_GROUP
    W = GROUP_W

    @pl.when(blk == 0)
    def _():
        state_ref[...] = jnp.zeros_like(state_ref)

    @pl.when(phase == 0)
    def _():
        def body(i, carry):
            r0 = pl.multiple_of(i * CHUNK, CHUNK)
            x = x_ref[0, pl.ds(r0, CHUNK), :].astype(F32)
            bm = b_ref[0, pl.ds(r0, CHUNK), :]
            e0 = jnp.dot(pack_ref[pl.ds(r0, CHUNK), :], sel0_ref[...],
                         preferred_element_type=F32)
            wa, ef = e0[:, :W], e0[:, W:]
            xa = (x * wa).astype(BF16)
            s_chunk = lax.dot_general(bm, xa, (((0,), (0,)), ((), ())),
                                      preferred_element_type=F32)
            st = state_ref[...]
            sprev_ref[blk * ncb + i] = st
            state_ref[...] = st * ef[CHUNK - 1:CHUNK, :] + s_chunk
            return carry

        lax.fori_loop(0, ncb, body, 0, unroll=SSD_UNROLL_FWD)

    @pl.when(phase == 1)
    def _():
        dskip = dskip_ref[0]
        nw = nw_ref[0]
        li = lax.broadcasted_iota(jnp.int32, (CHUNK, CHUNK), 0)
        si = lax.broadcasted_iota(jnp.int32, (CHUNK, CHUNK), 1)
        lower = li >= si
        upper = li <= si
        head_of_lane = lax.broadcasted_iota(jnp.int32, (CHUNK, W), 1) // SSM_HEAD_DIM

        def body(ii, carry):
            i = ncb - 1 - ii
            r0 = pl.multiple_of(i * CHUNK, CHUNK)
            xb16 = x_ref[0, pl.ds(r0, CHUNK), :]
            x = xb16.astype(F32)
            bm = b_ref[0, pl.ds(r0, CHUNK), :]
            cm = c_ref[0, pl.ds(r0, CHUNK), :]
            z = z_ref[0, pl.ds(r0, CHUNK), :].astype(F32)
            dtr = dtr_ref[0, :, pl.ds(r0, CHUNK)]
            cumr = cumr_ref[0, :, pl.ds(r0, CHUNK)]
            e1 = jnp.dot(pack_ref[pl.ds(r0, CHUNK), :], sel1_ref[...],
                         preferred_element_type=F32)
            ef, eb, wb = e1[:, :W], e1[:, W:2 * W], e1[:, 2 * W:3 * W]
            ccol = e1[:, 3 * W:]

            cb = lax.dot_general(cm, bm, (((1,), (1,)), ((), ())), preferred_element_type=F32)
            ms, xs = [], []
            for j in range(H):
                cf = ccol[:, j * CHUNK:(j + 1) * CHUNK]
                cbw = ccol[:, (H + j) * CHUNK:(H + j + 1) * CHUNK]
                df = jnp.exp(jnp.where(lower, cf - cumr[j:j + 1, :], NEG_BIG))
                db = jnp.exp(jnp.where(upper, cbw - cumr[H + j:H + j + 1, :], NEG_BIG))
                ms.append((cb * (df * dtr[j:j + 1, :] + db * dtr[H + j:H + j + 1, :])).astype(BF16))
                xs.append(jnp.where(head_of_lane == j, xb16, jnp.zeros_like(xb16)))
            y = jnp.dot(jnp.concatenate(ms, axis=1), jnp.concatenate(xs, axis=0),
                        preferred_element_type=F32)

            chunk_id = (nblk - 1 - blk) * ncb + i
            sp_f = sprev_ref[chunk_id].astype(BF16)
            st_b = state_ref[...]
            y = y + jnp.dot(cm, sp_f, preferred_element_type=F32) * ef
            y = y + jnp.dot(cm, st_b.astype(BF16), preferred_element_type=F32) * eb
            y = y + dskip * x

            xw = (x * wb).astype(BF16)
            s_chunk = lax.dot_general(bm, xw, (((0,), (0,)), ((), ())),
                                      preferred_element_type=F32)
            state_ref[...] = st_b * eb[0:1, :] + s_chunk

            yg = y * (z * (1.0 / (1.0 + jnp.exp(-z))))
            msq = jnp.mean(yg * yg, axis=-1, keepdims=True)
            o_ref[0, pl.ds(r0, CHUNK), :] = (yg * lax.rsqrt(msq + EPS) * nw).astype(o_ref.dtype)
            return carry

        lax.fori_loop(0, ncb, body, 0, unroll=SSD_UNROLL_BWD)


def _ssd(conv, big, pack, dtr, cumr, sel0, sel1, dskip, nw, tb):
    b, s, _ = conv.shape
    nblk = s // tb
    w = 2 * HEADS_PER_GROUP
    const = lambda bi, g, p, k: (0, 0)

    def sidx(p, k):
        return jnp.where(p == 0, k, nblk - 1 - k)

    def sidx_late(p, k):
        return jnp.where(p == 0, nblk - 1, nblk - 1 - k)

    b_col0 = SSM_D_INNER // SSM_D_STATE
    c_col0 = b_col0 + SSM_N_GROUPS
    return pl.pallas_call(
        _ssd_kernel,
        grid=(b, SSM_N_GROUPS, 2, nblk),
        in_specs=[
            pl.BlockSpec((1, tb, GROUP_W), lambda bi, g, p, k: (bi, sidx(p, k), g)),
            pl.BlockSpec((1, tb, SSM_D_STATE), lambda bi, g, p, k: (bi, sidx(p, k), b_col0 + g)),
            pl.BlockSpec((1, tb, SSM_D_STATE),
                         lambda bi, g, p, k: (bi, sidx_late(p, k), c_col0 + g)),
            pl.BlockSpec((1, tb, GROUP_W), lambda bi, g, p, k: (bi, sidx_late(p, k), g)),
            pl.BlockSpec((tb, LANES), lambda bi, g, p, k: (bi * nblk + sidx(p, k), g)),
            pl.BlockSpec((1, w, tb), lambda bi, g, p, k: (g, 0, bi * nblk + sidx_late(p, k))),
            pl.BlockSpec((1, w, tb), lambda bi, g, p, k: (g, 0, bi * nblk + sidx_late(p, k))),
            pl.BlockSpec(sel0.shape, const),
            pl.BlockSpec(sel1.shape, const),
            pl.BlockSpec((1, 1, GROUP_W), lambda bi, g, p, k: (g, 0, 0)),
            pl.BlockSpec((1, 1, GROUP_W), lambda bi, g, p, k: (g, 0, 0)),
        ],
        out_specs=pl.BlockSpec((1, tb, GROUP_W), lambda bi, g, p, k: (bi, sidx_late(p, k), g)),
        out_shape=jax.ShapeDtypeStruct((b, s, SSM_D_INNER), BF16),
        scratch_shapes=[
            pltpu.VMEM((s // CHUNK, SSM_D_STATE, GROUP_W), F32),
            pltpu.VMEM((SSM_D_STATE, GROUP_W), F32),
        ],
        compiler_params=_cparams(("parallel", "parallel", "arbitrary", "arbitrary")),
        name="ssd",
    )(conv, conv, conv, big, pack, dtr, cumr, sel0, sel1, dskip, nw)


def _rms(x, g):
    ms = jnp.mean(x * x, axis=-1, keepdims=True)
    return x * lax.rsqrt(ms + EPS) * g


def _mla_prep_kernel(ql_ref, kvl_ref, kra_ref, krb_ref, pos_ref, invf_ref,
                     gq_ref, gkv_ref, wq_ref, wqs_ref, wk_ref, wvt_ref, ones_ref,
                     eq_ref, ek_ref, q_ref, k_ref, vt_ref, nmax_ref):
    tm = ql_ref.shape[0]
    ang = invf_ref[...] * pos_ref[0]
    cos_t, sin_t = jnp.cos(ang), jnp.sin(ang)
    pad = jnp.zeros((HEAD_PAD - MLA_QK, tm), F32)
    cos = jnp.concatenate([jnp.ones((MLA_NOPE, tm), F32), cos_t, cos_t, pad], axis=0).T
    sin = jnp.concatenate([jnp.zeros((MLA_NOPE, tm), F32), -sin_t, sin_t, pad], axis=0).T
    scale = MLA_QK ** -0.5 * LOG2E
    cq = cos * scale
    sq = sin * scale
    qn = _rms(ql_ref[...], gq_ref[...]).astype(BF16)
    qa = jnp.dot(qn, wq_ref[...], preferred_element_type=F32)
    qb = jnp.dot(qn, wqs_ref[...], preferred_element_type=F32)
    kn = _rms(kvl_ref[...], gkv_ref[...]).astype(BF16)
    kk = jnp.dot(kn, wk_ref[...], preferred_element_type=F32)
    vvt = lax.dot_general(wvt_ref[...], kn, (((1,), (1,)), ((), ())),
                          preferred_element_type=F32) + ones_ref[...]
    kpe = kra_ref[...] * cos + krb_ref[...] * sin
    qsq, ksq = [], []
    for h in range(MLA_N_HEADS):
        sl = slice(h * HEAD_PAD, (h + 1) * HEAD_PAD)
        qh = (qa[:, sl] * cq + qb[:, sl] * sq).astype(BF16)
        kh = (kk[:, sl] + kpe).astype(BF16)
        q_ref[0, h] = qh
        k_ref[0, h] = kh
        qsq.append(qh * qh)
        ksq.append(kh * kh)
        vt_ref[0, h] = vvt[h * VT_ROWS:(h + 1) * VT_ROWS, :].astype(vt_ref.dtype)
    nsq = (jnp.dot(jnp.concatenate(qsq, axis=1), eq_ref[...], preferred_element_type=F32)
           + jnp.dot(jnp.concatenate(ksq, axis=1), ek_ref[...], preferred_element_type=F32))
    nmax_ref[0] = jnp.max(nsq.reshape(nsq.shape[0] // 8, 8, LANES), axis=0)


def _mla_prep(small, pos, invf, gq, gkv, wq, wqs, wk, wvt, ones, eq, ek, b, s, tm):
    t = small.shape[0]
    nsb = s // tm
    hw = MLA_N_HEADS * HEAD_PAD
    out = jax.ShapeDtypeStruct((b, MLA_N_HEADS, s, HEAD_PAD), BF16)
    out_vt = jax.ShapeDtypeStruct((b, MLA_N_HEADS, VT_ROWS, s), BF16)
    const = lambda i: (0, 0)
    ospec = pl.BlockSpec((1, MLA_N_HEADS, tm, HEAD_PAD), lambda i: (i // nsb, 0, i % nsb, 0))
    ospec_vt = pl.BlockSpec((1, MLA_N_HEADS, VT_ROWS, tm), lambda i: (i // nsb, 0, 0, i % nsb))
    return pl.pallas_call(
        _mla_prep_kernel,
        grid=(t // tm,),
        in_specs=[
            pl.BlockSpec((tm, MLA_Q_RANK), lambda i: (i, SM_Q // MLA_Q_RANK)),
            pl.BlockSpec((tm, MLA_KV_RANK), lambda i: (i, SM_KV // MLA_KV_RANK)),
            pl.BlockSpec((tm, LANES), lambda i: (i, SM_KRA // LANES)),
            pl.BlockSpec((tm, LANES), lambda i: (i, SM_KRB // LANES)),
            pl.BlockSpec((1, 1, tm), lambda i: (i // nsb, 0, i % nsb)),
            pl.BlockSpec((MLA_ROPE // 2, 1), const),
            pl.BlockSpec((1, MLA_Q_RANK), const),
            pl.BlockSpec((1, MLA_KV_RANK), const),
            pl.BlockSpec((MLA_Q_RANK, hw), const),
            pl.BlockSpec((MLA_Q_RANK, hw), const),
            pl.BlockSpec((MLA_KV_RANK, hw), const),
            pl.BlockSpec((MLA_N_HEADS * VT_ROWS, MLA_KV_RANK), const),
            pl.BlockSpec((MLA_N_HEADS * VT_ROWS, 1), const),
            pl.BlockSpec((hw, LANES), const),
            pl.BlockSpec((hw, LANES), const),
        ],
        out_specs=[ospec, ospec, ospec_vt, pl.BlockSpec((1, 8, LANES), lambda i: (i, 0, 0))],
        out_shape=[out, out, out_vt, jax.ShapeDtypeStruct((t // tm, 8, LANES), F32)],
        compiler_params=_cparams(("parallel",)),
        name="mla_prep",
    )(small, small, small, small, pos, invf, gq, gkv, wq, wqs, wk, wvt, ones, eq, ek)


def _attn_kernel(nmax_ref, q_ref, k_ref, vt_ref, o_ref, s_ref, *, tq, tk, unroll):
    s = k_ref.shape[2]
    n_kv = s // tk
    n_q = s // tq
    n_tiles = n_q * n_kv
    assert n_kv % unroll == 0 and unroll % 2 == 0

    def finish(accs, qi):
        outs = [a[:MLA_V] / a[VT_ONES_ROW:VT_ONES_ROW + 1] for a in accs]
        q0 = pl.multiple_of(qi * tq, tq)
        o_ref[0, pl.ds(q0, tq), :] = jnp.concatenate(outs, axis=0).T.astype(o_ref.dtype)

    hp = pl.program_id(1)
    nmax = jnp.max(nmax_ref[...], axis=(0, 1), keepdims=True)[0]
    lane = lax.broadcasted_iota(jnp.int32, (1, LANES), 1)
    bound_sq = []
    for hh in range(2):
        qn2 = jnp.max(jnp.where(lane == 2 * hp + hh, nmax, 0.0))
        kn2 = jnp.max(jnp.where(lane == MLA_N_HEADS + 2 * hp + hh, nmax, 0.0))
        bound_sq.append(qn2 * kn2)
    small_scores = jnp.maximum(bound_sq[0], bound_sq[1]) <= SCORE_BOUND ** 2

    @pl.when(small_scores)
    def _():
        _attn_pipeline(q_ref, k_ref, vt_ref, s_ref, finish, tq=tq, tk=tk, unroll=unroll,
                       use_max=False)

    @pl.when(jnp.logical_not(small_scores))
    def _():
        _attn_pipeline(q_ref, k_ref, vt_ref, s_ref, finish, tq=tq, tk=tk, unroll=unroll,
                       use_max=True)


def _attn_pipeline(q_ref, k_ref, vt_ref, s_ref, finish, *, tq, tk, unroll, use_max):
    s = k_ref.shape[2]
    n_kv = s // tk
    n_tiles = (s // tq) * n_kv

    def scores(t, slot):
        t = jnp.minimum(t, n_tiles - 1)
        q0 = pl.multiple_of((t // n_kv) * tq, tq)
        r0 = pl.multiple_of((t % n_kv) * tk, tk)
        for hh in range(2):
            s_ref[slot, hh] = lax.dot_general(k_ref[0, hh, pl.ds(r0, tk), :],
                                              q_ref[0, hh, pl.ds(q0, tq), :],
                                              (((1,), (1,)), ((), ())),
                                              preferred_element_type=F32)

    def consume(ki, slot, stats):
        r0 = pl.multiple_of(ki * tk, tk)
        new = []
        for hh in range(2):
            m, acc = stats[2 * hh], stats[2 * hh + 1]
            vt = vt_ref[0, hh, :, pl.ds(r0, tk)]
            if use_max:
                m_new = jnp.maximum(m, jnp.max(s_ref[slot, hh], axis=0, keepdims=True))
                p = jnp.exp2(s_ref[slot, hh] - m_new).astype(BF16)
                acc = acc * jnp.exp2(m - m_new)
            else:
                m_new = m
                p = jnp.exp2(s_ref[slot, hh]).astype(BF16)
            acc = acc + jnp.dot(vt, p, preferred_element_type=F32)
            new += [m_new, acc]
        return tuple(new)

    m0 = jnp.full((1, tq), NEG_BIG, F32)
    acc0 = jnp.zeros((VT_ROWS, tq), F32)
    init = (m0, acc0, m0, acc0)

    def trip(j, stats):
        t0 = j * unroll
        qi = t0 // n_kv
        k0 = t0 % n_kv
        stats = tuple(jnp.where(k0 == 0, a, b) for a, b in zip(init, stats))
        for u in range(unroll):
            scores(t0 + u + 1, (u + 1) % 2)
            stats = consume(k0 + u, u % 2, stats)

        @pl.when(k0 + unroll == n_kv)
        def _():
            finish((stats[1], stats[3]), qi)

        return stats

    scores(0, 0)
    lax.fori_loop(0, n_tiles // unroll, trip, init)


def _attention(nmax, q, k, vt, tq, tk, unroll):
    b, h, s, _ = q.shape
    nt = nmax.shape[0] // b
    return pl.pallas_call(
        functools.partial(_attn_kernel, tq=tq, tk=tk, unroll=unroll),
        grid=(b, h // 2),
        in_specs=[
            pl.BlockSpec((nt, 8, LANES), lambda bi, hp: (bi, 0, 0)),
            pl.BlockSpec((1, 2, s, HEAD_PAD), lambda bi, hp: (bi, hp, 0, 0)),
            pl.BlockSpec((1, 2, s, HEAD_PAD), lambda bi, hp: (bi, hp, 0, 0)),
            pl.BlockSpec((1, 2, VT_ROWS, s), lambda bi, hp: (bi, hp, 0, 0)),
        ],
        out_specs=pl.BlockSpec((1, s, 2 * MLA_V), lambda bi, hp: (bi, 0, hp)),
        out_shape=jax.ShapeDtypeStruct((b, s, h * MLA_V), BF16),
        scratch_shapes=[pltpu.VMEM((2, 2, tk, tq), F32)],
        compiler_params=_cparams(("parallel", "parallel")),
        name="attention",
    )(nmax, q, k, vt)


def _merge_kernel(x_ref, y_ref, a_ref, gs_ref, gm_ref, gb_ref, wso_ref, wmo_ref, wo_ref,
                  nw_ref, o_ref):
    y_ssm = jnp.dot(y_ref[...], wso_ref[...], preferred_element_type=F32)
    y_mla = jnp.dot(a_ref[...], wmo_ref[...], preferred_element_type=F32)
    gb = gb_ref[...]
    g_ssm = 1.0 / (1.0 + jnp.exp(-(gs_ref[...].astype(F32) + gb[:, :D_MODEL])))
    g_mla = 1.0 / (1.0 + jnp.exp(-(gm_ref[...].astype(F32) + gb[:, D_MODEL:])))
    mix = (g_ssm * y_ssm + g_mla * y_mla).astype(BF16)
    mixed = jnp.dot(mix, wo_ref[...], preferred_element_type=F32)
    o_ref[...] = x_ref[...] + _rms(mixed, nw_ref[...])


def _merge(x, y, attn, big, gate_b, wso, wmo, wo, nw, tm):
    t, d = x.shape
    const = lambda i: (0, 0)
    g0 = BIG_GATE // D_MODEL
    return pl.pallas_call(
        _merge_kernel,
        grid=(t // tm,),
        in_specs=[
            pl.BlockSpec((tm, d), lambda i: (i, 0)),
            pl.BlockSpec((tm, SSM_D_INNER), lambda i: (i, 0)),
            pl.BlockSpec((tm, d), lambda i: (i, 0)),
            pl.BlockSpec((tm, d), lambda i: (i, g0)),
            pl.BlockSpec((tm, d), lambda i: (i, g0 + 1)),
            pl.BlockSpec((1, 2 * d), const),
            pl.BlockSpec((SSM_D_INNER, d), const),
            pl.BlockSpec((d, d), const),
            pl.BlockSpec((d, d), const),
            pl.BlockSpec((1, d), const),
        ],
        out_specs=pl.BlockSpec((tm, d), lambda i: (i, 0)),
        out_shape=jax.ShapeDtypeStruct((t, d), F32),
        compiler_params=_cparams(("parallel",)),
        name="merge",
    )(x, y, attn, big, big, gate_b, wso, wmo, wo, nw)


def _mlp_kernel(x_ref, g1_ref, wu_ref, wd_ref, g2_ref, o_ref, h_ref, acc_ref):
    j = pl.program_id(1)

    @pl.when(j == 0)
    def _():
        h_ref[...] = _rms(x_ref[...], g1_ref[...]).astype(h_ref.dtype)
        acc_ref[...] = jnp.zeros_like(acc_ref)

    u = jnp.dot(h_ref[...], wu_ref[...], preferred_element_type=F32)
    r = jnp.maximum(u, 0.0)
    acc_ref[...] += jnp.dot((r * r).astype(BF16), wd_ref[...], preferred_element_type=F32)

    @pl.when(j == pl.num_programs(1) - 1)
    def _():
        o_ref[...] = x_ref[...] + _rms(acc_ref[...], g2_ref[...])


def _mlp(x, g1, wu, wd, g2, tm, tf):
    t, d = x.shape
    f = wu.shape[1]
    return pl.pallas_call(
        _mlp_kernel,
        grid=(t // tm, f // tf),
        in_specs=[
            pl.BlockSpec((tm, d), lambda i, j: (i, 0)),
            pl.BlockSpec((1, d), lambda i, j: (0, 0)),
            pl.BlockSpec((d, tf), lambda i, j: (0, j)),
            pl.BlockSpec((tf, d), lambda i, j: (j, 0)),
            pl.BlockSpec((1, d), lambda i, j: (0, 0)),
        ],
        out_specs=pl.BlockSpec((tm, d), lambda i, j: (i, 0)),
        out_shape=jax.ShapeDtypeStruct((t, d), F32),
        scratch_shapes=[pltpu.VMEM((tm, d), BF16), pltpu.VMEM((tm, d), F32)],
        compiler_params=_cparams(("parallel", "arbitrary")),
        name="mlp",
    )(x, g1, wu, wd, g2)


def _group_major(fwd, bwd):
    lead = fwd.shape[:-1]
    f = fwd.reshape(lead + (SSM_N_GROUPS, HEADS_PER_GROUP))
    bw = bwd.reshape(lead + (SSM_N_GROUPS, HEADS_PER_GROUP))
    return jnp.concatenate([f, bw], axis=-1).reshape(lead + (2 * SSM_N_HEADS,))


def _pad_cols(w, n):
    return jnp.pad(w, ((0, 0), (0, n - w.shape[1])))


def _pick(n, prefs):
    for p in prefs:
        if n % p == 0:
            return p
    raise ValueError(f"no tile in {prefs} divides {n}")


def kernel(x, positions, norm_mix_pre, w_in, conv_w, conv_b, dt_bias_fwd, dt_bias_bwd, a_log_fwd, a_log_bwd, d_skip, ssm_norm_w, w_ssm_out, q_a_norm, w_q_b, kv_a_norm, w_kv_b, w_mla_out, gate_b, w_out, norm_mix_post, norm_mlp_pre, w_up, w_down, norm_mlp_post):
    b, s, d = x.shape
    t = b * s
    depth = w_in.shape[0]
    assert d == D_MODEL and s % 512 == 0
    half = MLA_ROPE // 2

    inv_freq = ROPE_BASE ** (-np.arange(0, MLA_ROPE, 2, dtype=np.float32) / MLA_ROPE)
    invf = inv_freq.reshape(half, 1)
    ones = np.zeros((MLA_N_HEADS, VT_ROWS, 1), np.float32)
    ones[:, VT_ONES_ROW] = 1.0
    ones = ones.reshape(MLA_N_HEADS * VT_ROWS, 1)
    head_of_col = np.arange(MLA_N_HEADS * HEAD_PAD) // HEAD_PAD
    eq = (head_of_col[:, None] == np.arange(LANES)[None, :]).astype(np.float32)
    ek = (head_of_col[:, None] + MLA_N_HEADS == np.arange(LANES)[None, :]).astype(np.float32)
    eq, ek = jnp.asarray(eq, BF16), jnp.asarray(ek, BF16)
    invf, ones = jnp.asarray(invf), jnp.asarray(ones)
    perm, sel0, sel1 = (jnp.asarray(m, BF16) for m in _pack_constants())
    pos = positions.astype(F32).reshape(b, 1, s)

    xt = x.reshape(t, d)
    for l in range(depth):
        wz, wxbc, wdt, wql, wkvl, wkr, wg = jnp.split(
            w_in[l], np.cumsum([SSM_D_INNER, SSM_CONV_DIM, 2 * SSM_N_HEADS, MLA_Q_RANK,
                                MLA_KV_RANK, MLA_ROPE])[:].tolist(), axis=1)
        w_big = jnp.concatenate([wz, wxbc, wg], axis=1).astype(BF16)
        wdt_gm = _group_major(wdt[:, :SSM_N_HEADS], wdt[:, SSM_N_HEADS:])
        wkr_sw = jnp.concatenate([wkr[:, half:], wkr[:, :half]], axis=1)
        zpad = jnp.zeros((d, MLA_NOPE), F32)
        w_small = jnp.concatenate([
            wql, wkvl,
            _pad_cols(jnp.concatenate([zpad, wkr], axis=1), LANES),
            _pad_cols(jnp.concatenate([zpad, wkr_sw], axis=1), LANES),
            _pad_cols(wdt_gm, LANES)], axis=1).astype(BF16)
        dt_bias = _pad_cols(_group_major(dt_bias_fwd[l], dt_bias_bwd[l])[None, :], LANES)
        a_log = _pad_cols(_group_major(a_log_fwd[l], a_log_bwd[l])[None, :], LANES)
        dskip = jnp.repeat(d_skip[l], SSM_HEAD_DIM).reshape(SSM_N_GROUPS, 1, GROUP_W)
        nw_ssm = ssm_norm_w[l].reshape(SSM_N_GROUPS, 1, GROUP_W)

        wq3 = w_q_b[l].reshape(MLA_Q_RANK, MLA_N_HEADS, MLA_QK)
        q_nope, q_pe = wq3[..., :MLA_NOPE], wq3[..., MLA_NOPE:]
        q_pe_sw = jnp.concatenate([q_pe[..., half:], q_pe[..., :half]], axis=-1)
        hz = jnp.zeros((MLA_Q_RANK, MLA_N_HEADS, HEAD_PAD - MLA_QK), F32)
        wq = jnp.concatenate([q_nope, q_pe, hz], axis=-1).reshape(MLA_Q_RANK, -1).astype(BF16)
        wqs = jnp.concatenate([jnp.zeros_like(q_nope), q_pe_sw, hz],
                              axis=-1).reshape(MLA_Q_RANK, -1).astype(BF16)
        wkv3 = w_kv_b[l].reshape(MLA_KV_RANK, MLA_N_HEADS, MLA_NOPE + MLA_V)
        k_nope, v_w = wkv3[..., :MLA_NOPE], wkv3[..., MLA_NOPE:]
        z64 = jnp.zeros_like(k_nope)
        wk = jnp.concatenate([k_nope, z64], axis=-1).reshape(MLA_KV_RANK, -1).astype(BF16)
        wvt = jnp.pad(jnp.transpose(v_w, (1, 2, 0)), ((0, 0), (0, VT_ROWS - MLA_V), (0, 0)))
        wvt = wvt.reshape(MLA_N_HEADS * VT_ROWS, MLA_KV_RANK).astype(BF16)

        tm = _pick(t, (1024, 512))
        gain = norm_mix_pre[l][None, :]
        big, small = _in_proj(xt, gain, w_big, w_small, tm, 2048)

        pack, dtr, cumr = _dt_prep(small, dt_bias, a_log, perm, _pick(t, (1024, 512)))
        big3 = big.reshape(b, s, BIG_W)
        conv = _conv(big3, conv_w[l], conv_b[l][None, :], _pick(s, (1024, 512)), 512)
        y = _ssd(conv, big3, pack, dtr, cumr, sel0, sel1, dskip, nw_ssm,
                 _pick(s, (2048, 1024)))

        q, k, vt, nmax = _mla_prep(small, pos, invf, q_a_norm[l][None, :],
                                   kv_a_norm[l][None, :], wq, wqs, wk, wvt, ones, eq, ek, b, s,
                                   _pick(s, (1024, 512)))
        attn = _attention(nmax, q, k, vt, ATT_TQ, ATT_TK, ATT_UNROLL)

        x1 = _merge(xt, y.reshape(t, SSM_D_INNER), attn.reshape(t, d), big,
                    gate_b[l][None, :], w_ssm_out[l].astype(BF16), w_mla_out[l].astype(BF16),
                    w_out[l].astype(BF16), norm_mix_post[l][None, :], 512)

        xt = _mlp(x1, norm_mlp_pre[l][None, :], w_up[l].astype(BF16), w_down[l].astype(BF16),
                  norm_mlp_post[l][None, :], tm, 1024)
    return xt.reshape(b, s, d)
```
